```python
import math
import jax, jax.numpy as jnp
from jax import lax
import numpy as np

D_MODEL = 1024
BATCH = 16
SEQ = 2048
DEPTH = 2

POOL_WIDTH = D_MODEL // 2
POOL_WINDOWS = (2, 4, 8, 16)
N_POOL_GROUPS = len(POOL_WINDOWS)
POOL_GROUP = POOL_WIDTH // N_POOL_GROUPS
MAX_POOL_WINDOW = max(POOL_WINDOWS)
HEAD_DIM = 64
N_Q_HEADS = (D_MODEL - POOL_WIDTH) // HEAD_DIM
N_KV_HEADS = 2
GQA_GROUP = N_Q_HEADS // N_KV_HEADS
ATTN_WIDTH = N_Q_HEADS * HEAD_DIM
KV_WIDTH = N_KV_HEADS * HEAD_DIM
WINDOW = 128
BLOCK = 128
N_BUCKETS = 32
MAX_DISTANCE = 128
MIX_WIDTH = POOL_WIDTH + ATTN_WIDTH
IN_SPLITS = (POOL_WIDTH, 2 * POOL_WIDTH, 2 * POOL_WIDTH + ATTN_WIDTH,
             2 * POOL_WIDTH + ATTN_WIDTH + KV_WIDTH, 2 * POOL_WIDTH + ATTN_WIDTH + 2 * KV_WIDTH)
IN_WIDTH = 2 * POOL_WIDTH + 2 * ATTN_WIDTH + 2 * KV_WIDTH
DEEPNORM_ALPHA = (2.0 * DEPTH) ** 0.25
DEEPNORM_BETA = (8.0 * DEPTH) ** -0.25
LN_EPS = 1e-5
NEG_INF = -1e30

kernel_name = "hymba_pool_swa_sink_deepnorm"


def layer_norm(x, g, b):
    xf = x.astype(jnp.float32)
    mu = jnp.mean(xf, axis=-1, keepdims=True)
    var = jnp.mean(jnp.square(xf - mu), axis=-1, keepdims=True)
    y = (xf - mu) * lax.rsqrt(var + LN_EPS) * g.astype(jnp.float32) + b.astype(jnp.float32)
    return y.astype(x.dtype)


def t5_bucket(dist):
    max_exact = N_BUCKETS // 2
    is_small = dist < max_exact
    large = max_exact + (jnp.log(jnp.maximum(dist, 1).astype(jnp.float32) / max_exact)
                         / math.log(MAX_DISTANCE / max_exact) * (N_BUCKETS - max_exact)).astype(jnp.int32)
    large = jnp.minimum(large, N_BUCKETS - 1)
    return jnp.where(is_small, dist, large)


def multiscale_pool(u, pool_w, pool_scale):
    B, S, _ = u.shape
    uf = u.astype(jnp.float32)
    cs = jnp.pad(jnp.cumsum(uf, axis=1), ((0, 0), (MAX_POOL_WINDOW, 0), (0, 0)))
    t = jnp.arange(S)
    groups = []
    for gi, w in enumerate(POOL_WINDOWS):
        sl = slice(gi * POOL_GROUP, (gi + 1) * POOL_GROUP)
        window_sum = cs[:, MAX_POOL_WINDOW:, sl] - cs[:, MAX_POOL_WINDOW - w:MAX_POOL_WINDOW - w + S, sl]
        count = jnp.minimum(t + 1, w).astype(jnp.float32)[None, :, None]
        groups.append(window_sum / count - uf[..., sl])
    pooled = jnp.stack(groups, axis=2).astype(u.dtype)
    mixed = jnp.einsum('bsgc,gcd->bsgd', pooled, pool_w).reshape(B, S, POOL_WIDTH)
    return mixed * pool_scale


def sliding_window_gqa(q, k, v, sinks, rel_bias):
    B, S = q.shape[:2]
    nb = S // BLOCK
    qb = q.reshape(B, nb, BLOCK, N_KV_HEADS, GQA_GROUP, HEAD_DIM)
    kpad = jnp.pad(k, ((0, 0), (BLOCK, 0), (0, 0), (0, 0)))
    vpad = jnp.pad(v, ((0, 0), (BLOCK, 0), (0, 0), (0, 0)))
    shape_blk = (B, nb, BLOCK, N_KV_HEADS, HEAD_DIM)
    kb = jnp.concatenate([kpad[:, :S].reshape(shape_blk), kpad[:, BLOCK:].reshape(shape_blk)], axis=2)
    vb = jnp.concatenate([vpad[:, :S].reshape(shape_blk), vpad[:, BLOCK:].reshape(shape_blk)], axis=2)
    scores = jnp.einsum('bnqhgd,bnkhd->bnhgqk', qb, kb,
                        preferred_element_type=jnp.float32) * (HEAD_DIM ** -0.5)
    qi = jnp.arange(BLOCK)[:, None]
    kj = jnp.arange(2 * BLOCK)[None, :]
    dist = qi + BLOCK - kj
    blk = jnp.arange(nb)[:, None, None]
    key_abs = (blk - 1) * BLOCK + kj[None]
    valid = (dist[None] >= 0) & (dist[None] < WINDOW) & (key_abs >= 0)
    bias = rel_bias.astype(jnp.float32)[t5_bucket(jnp.clip(dist, 0, None))]
    bias = jnp.transpose(bias, (2, 0, 1)).reshape(N_KV_HEADS, GQA_GROUP, BLOCK, 2 * BLOCK)
    scores = jnp.where(valid[None, :, None, None], scores + bias[None, None], NEG_INF)
    sink = jnp.broadcast_to(sinks.astype(jnp.float32).reshape(1, 1, N_KV_HEADS, GQA_GROUP, 1, 1),
                            scores.shape[:-1] + (1,))
    probs = jax.nn.softmax(jnp.concatenate([scores, sink], axis=-1), axis=-1)[..., :-1]
    out = jnp.einsum('bnhgqk,bnkhd->bnqhgd', probs.astype(v.dtype), vb)
    return out.reshape(B, S, ATTN_WIDTH)


def hybrid_layer(x, ln_g, ln_b, w_in, pool_w, pool_scale, sinks, w_out, rel_bias):
    B, S, _ = x.shape
    h = x @ w_in
    u_pool, g_pool, q, k, v, g_attn = jnp.split(h, IN_SPLITS, axis=-1)
    y_pool = multiscale_pool(u_pool, pool_w, pool_scale) * jax.nn.silu(g_pool)
    y_attn = sliding_window_gqa(q.reshape(B, S, N_Q_HEADS, HEAD_DIM),
                                k.reshape(B, S, N_KV_HEADS, HEAD_DIM),
                                v.reshape(B, S, N_KV_HEADS, HEAD_DIM),
                                sinks, rel_bias) * jax.nn.silu(g_attn)
    y = jnp.concatenate([y_pool, y_attn], axis=-1) @ w_out
    return layer_norm(DEEPNORM_ALPHA * x + y, ln_g, ln_b)


def setup_inputs(seed: int = 0) -> dict:
    key = jax.random.key(seed)
    ks = jax.random.split(key, 10)
    f32 = jnp.float32
    x = jax.random.normal(ks[0], (BATCH, SEQ, D_MODEL), f32)
    ln_g = 1.0 + 0.02 * jax.random.normal(ks[1], (DEPTH, D_MODEL), f32)
    ln_b = 0.02 * jax.random.normal(ks[2], (DEPTH, D_MODEL), f32)
    col_scale = jnp.ones((IN_WIDTH,), f32).at[IN_SPLITS[3]:IN_SPLITS[4]].set(DEEPNORM_BETA)
    w_in = jax.random.normal(ks[3], (DEPTH, D_MODEL, IN_WIDTH), f32) * (D_MODEL ** -0.5) * col_scale
    pool_w = jax.random.normal(ks[4], (DEPTH, N_POOL_GROUPS, POOL_GROUP, POOL_GROUP), f32) * (POOL_GROUP ** -0.5)
    pool_scale = 1.0 + 0.02 * jax.random.normal(ks[5], (DEPTH, POOL_WIDTH), f32)
    sinks = 0.5 * jax.random.normal(ks[6], (DEPTH, N_Q_HEADS), f32)
    w_out = jax.random.normal(ks[7], (DEPTH, MIX_WIDTH, D_MODEL), f32) * (MIX_WIDTH ** -0.5) * DEEPNORM_BETA
    rel_bias = 0.5 * jax.random.normal(ks[8], (N_BUCKETS, N_Q_HEADS), f32)
    return {"x": x, "ln_g": ln_g, "ln_b": ln_b, "w_in": w_in, "pool_w": pool_w,
            "pool_scale": pool_scale, "sinks": sinks, "w_out": w_out, "rel_bias": rel_bias}


def reference(x, ln_g, ln_b, w_in, pool_w, pool_scale, sinks, w_out, rel_bias):
    for layer in range(DEPTH):
        x = hybrid_layer(x, ln_g[layer], ln_b[layer], w_in[layer], pool_w[layer],
                         pool_scale[layer], sinks[layer], w_out[layer], rel_bias)
    return x
```

```python
import functools
import math

import jax
import jax.numpy as jnp
from jax import lax
from jax.experimental import pallas as pl
from jax.experimental.pallas import tpu as pltpu

D_MODEL = 1024
DEPTH = 2
POOL_WIDTH = 512
POOL_WINDOWS = (2, 4, 8, 16)
POOL_GROUP = 128
MAX_POOL_WINDOW = 16
HEAD_DIM = 64
N_Q_HEADS = 8
N_KV_HEADS = 2
GQA_GROUP = 4
ATTN_WIDTH = 512
KV_WIDTH = 128
WINDOW = 128
BLOCK = 128
N_BUCKETS = 32
MAX_DISTANCE = 128
IN_WIDTH = 2304
COL_U, COL_GP, COL_Q, COL_K, COL_V, COL_GA = 0, 512, 1024, 1536, 1664, 1792
DEEPNORM_ALPHA = (2.0 * DEPTH) ** 0.25
LN_EPS = 1e-5
NEG_INF = -1e30

TILE = 512
N_CHUNK = 256
LN_ROWS = 64
VMEM_LIMIT_BYTES = 56 * 1024 * 1024

F32 = jnp.float32
BF16 = jnp.bfloat16


def _silu(v):
    return v / (1.0 + jnp.exp(-v))


def _layer_kernel(sinks_ref, x_ref, w_in_ref, pw_ref, pscale_ref, w_out_ref, tab_ref,
                  ln_g_ref, ln_b_ref, o_ref,
                  xb_scr, uext_scr, kext_scr, vext_scr, q_scr, yattn_scr, y_scr):
    s_idx = pl.program_id(1)
    T = TILE
    H = MAX_POOL_WINDOW

    @pl.when(s_idx == 0)
    def _():
        uext_scr[0:H, :] = jnp.zeros((H, POOL_WIDTH), F32)
        kext_scr[:, 0:BLOCK, :] = jnp.zeros((N_KV_HEADS, BLOCK, HEAD_DIM), BF16)
        vext_scr[:, 0:BLOCK, :] = jnp.zeros((N_KV_HEADS, BLOCK, HEAD_DIM), BF16)

    xb_scr[...] = x_ref[...].astype(BF16)

    def proj(col, width):
        return jnp.dot(xb_scr[...], w_in_ref[:, col:col + width], preferred_element_type=F32)

    t_abs = s_idx * T + lax.broadcasted_iota(jnp.int32, (T, POOL_GROUP), 0)
    for p in range(POOL_WIDTH // N_CHUNK):
        c0 = p * N_CHUNK
        uext_scr[H:H + T, c0:c0 + N_CHUNK] = proj(COL_U + c0, N_CHUNK)
        pooled = []
        for gi in range(2 * p, 2 * p + 2):
            w = POOL_WINDOWS[gi]
            g0 = gi * POOL_GROUP
            u = uext_scr[H:H + T, g0:g0 + POOL_GROUP]
            acc = u
            for j in range(1, w):
                acc = acc + uext_scr[H - j:H - j + T, g0:g0 + POOL_GROUP]
            count = jnp.minimum(t_abs + 1, w).astype(F32)
            pooled.append((acc / count - u).astype(BF16))
        pooled = jnp.concatenate(pooled, axis=1)
        mixed = jnp.dot(pooled, pw_ref[p], preferred_element_type=F32)
        gate = _silu(proj(COL_GP + c0, N_CHUNK))
        y_scr[:, c0:c0 + N_CHUNK] = (mixed * pscale_ref[:, c0:c0 + N_CHUNK] * gate).astype(BF16)
    uext_scr[0:H, :] = uext_scr[T:T + H, :]

    for c in range(ATTN_WIDTH // N_CHUNK):
        q = (proj(COL_Q + c * N_CHUNK, N_CHUNK) * (HEAD_DIM ** -0.5)).astype(BF16)
        for g in range(GQA_GROUP):
            q_scr[c * GQA_GROUP + g] = q[:, g * HEAD_DIM:(g + 1) * HEAD_DIM]
    kv = proj(COL_K, 2 * KV_WIDTH).astype(BF16)
    for h in range(N_KV_HEADS):
        kext_scr[h, BLOCK:BLOCK + T, :] = kv[:, h * HEAD_DIM:(h + 1) * HEAD_DIM]
        vext_scr[h, BLOCK:BLOCK + T, :] = kv[:, KV_WIDTH + h * HEAD_DIM:KV_WIDTH + (h + 1) * HEAD_DIM]

    first = (s_idx == 0).astype(jnp.int32)
    for h in range(N_KV_HEADS):
        sink = [sinks_ref[h * GQA_GROUP + g] for g in range(GQA_GROUP)]
        for i in range(T // BLOCK):
            r0 = i * BLOCK
            qs = q_scr[h * GQA_GROUP:(h + 1) * GQA_GROUP, r0:r0 + BLOCK, :]
            qs = qs.reshape(GQA_GROUP * BLOCK, HEAD_DIM)
            kwin = kext_scr[h, r0:r0 + 2 * BLOCK, :]
            vwin = vext_scr[h, r0:r0 + 2 * BLOCK, :]
            sc = lax.dot_general(qs, kwin, (((1,), (1,)), ((), ())),
                                 preferred_element_type=F32)
            sel = first if i == 0 else 0
            probs = []
            inv_l = []
            for g in range(GQA_GROUP):
                sg = sc[g * BLOCK:(g + 1) * BLOCK, :] + tab_ref[sel, h * GQA_GROUP + g]
                m = jnp.maximum(jnp.max(sg, axis=-1, keepdims=True), sink[g])
                e = jnp.exp(sg - m)
                l = jnp.sum(e, axis=-1, keepdims=True) + jnp.exp(sink[g] - m)
                probs.append(e.astype(BF16))
                inv_l.append(1.0 / l)
            probs = jnp.concatenate(probs, axis=0)
            o = jnp.dot(probs, vwin, preferred_element_type=F32)
            for g in range(GQA_GROUP):
                col = (h * GQA_GROUP + g) * HEAD_DIM
                yattn_scr[r0:r0 + BLOCK, col:col + HEAD_DIM] = o[g * BLOCK:(g + 1) * BLOCK, :] * inv_l[g]
    kext_scr[:, 0:BLOCK, :] = kext_scr[:, T:T + BLOCK, :]
    vext_scr[:, 0:BLOCK, :] = vext_scr[:, T:T + BLOCK, :]

    for c in range(ATTN_WIDTH // N_CHUNK):
        c0 = c * N_CHUNK
        gate = _silu(proj(COL_GA + c0, N_CHUNK))
        y_scr[:, POOL_WIDTH + c0:POOL_WIDTH + c0 + N_CHUNK] = (
            yattn_scr[:, c0:c0 + N_CHUNK] * gate).astype(BF16)

    o_ref[...] = jnp.dot(y_scr[...], w_out_ref[...], preferred_element_type=F32)
    gamma = ln_g_ref[...]
    beta = ln_b_ref[...]
    for r in range(T // LN_ROWS):
        rows = slice(r * LN_ROWS, (r + 1) * LN_ROWS)
        z = DEEPNORM_ALPHA * x_ref[rows, :] + o_ref[rows, :]
        mu = jnp.mean(z, axis=-1, keepdims=True)
        zc = z - mu
        var = jnp.mean(zc * zc, axis=-1, keepdims=True)
        o_ref[rows, :] = zc * lax.rsqrt(var + LN_EPS) * gamma + beta


def _t5_bucket(dist):
    max_exact = N_BUCKETS // 2
    is_small = dist < max_exact
    large = max_exact + (jnp.log(jnp.maximum(dist, 1).astype(F32) / max_exact)
                         / math.log(MAX_DISTANCE / max_exact) * (N_BUCKETS - max_exact)).astype(jnp.int32)
    large = jnp.minimum(large, N_BUCKETS - 1)
    return jnp.where(is_small, dist, large)


def _bias_tables(rel_bias):
    qi = jnp.arange(BLOCK)[:, None]
    kj = jnp.arange(2 * BLOCK)[None, :]
    dist = qi + BLOCK - kj
    valid = (dist >= 0) & (dist < WINDOW)
    bias = rel_bias.astype(F32)[_t5_bucket(jnp.clip(dist, 0, None))]
    bias = jnp.transpose(bias, (2, 0, 1))
    tab_any = jnp.where(valid[None], bias, NEG_INF)
    tab_first = jnp.where((valid & (kj >= BLOCK))[None], bias, NEG_INF)
    return jnp.stack([tab_any, tab_first])


def _pair_block_diag(pool_w):
    z = jnp.zeros((POOL_GROUP, POOL_GROUP), pool_w.dtype)
    pairs = []
    for p in range(2):
        a, b = pool_w[2 * p], pool_w[2 * p + 1]
        pairs.append(jnp.concatenate([jnp.concatenate([a, z], axis=1),
                                      jnp.concatenate([z, b], axis=1)], axis=0))
    return jnp.stack(pairs)


def _layer(x, ln_g, ln_b, w_in, pool_w, pool_scale, sinks, w_out, tab):
    B, S, D = x.shape
    T = TILE
    const = lambda *shape: pl.BlockSpec(shape, lambda b, s: (0,) * len(shape))
    return pl.pallas_call(
        _layer_kernel,
        out_shape=jax.ShapeDtypeStruct((B, S, D), F32),
        grid=(B, S // T),
        in_specs=[
            pl.BlockSpec(memory_space=pltpu.SMEM),
            pl.BlockSpec((None, T, D), lambda b, s: (b, s, 0)),
            const(D, IN_WIDTH),
            const(2, N_CHUNK, N_CHUNK),
            const(1, POOL_WIDTH),
            const(D, D),
            const(2, N_Q_HEADS, BLOCK, 2 * BLOCK),
            const(1, D),
            const(1, D),
        ],
        out_specs=pl.BlockSpec((None, T, D), lambda b, s: (b, s, 0)),
        scratch_shapes=[
            pltpu.VMEM((T, D), BF16),
            pltpu.VMEM((MAX_POOL_WINDOW + T, POOL_WIDTH), F32),
            pltpu.VMEM((N_KV_HEADS, BLOCK + T, HEAD_DIM), BF16),
            pltpu.VMEM((N_KV_HEADS, BLOCK + T, HEAD_DIM), BF16),
            pltpu.VMEM((N_Q_HEADS, T, HEAD_DIM), BF16),
            pltpu.VMEM((T, ATTN_WIDTH), F32),
            pltpu.VMEM((T, D), BF16),
        ],
        compiler_params=pltpu.CompilerParams(
            dimension_semantics=("arbitrary", "arbitrary"),
            vmem_limit_bytes=VMEM_LIMIT_BYTES),
        name="hybrid_layer",
    )(sinks, x, w_in.astype(BF16), _pair_block_diag(pool_w).astype(BF16),
      pool_scale.reshape(1, POOL_WIDTH), w_out.astype(BF16), tab,
      ln_g.reshape(1, D), ln_b.reshape(1, D))


@jax.jit
def kernel(x, ln_g, ln_b, w_in, pool_w, pool_scale, sinks, w_out, rel_bias):
    tab = _bias_tables(rel_bias)
    for layer in range(DEPTH):
        x = _layer(x, ln_g[layer], ln_b[layer], w_in[layer], pool_w[layer],
                   pool_scale[layer], sinks[layer], w_out[layer], tab)
    return x
```

```python
import math

import jax
import jax.numpy as jnp
from jax import lax
from jax.experimental import pallas as pl
from jax.experimental.pallas import tpu as pltpu

D_MODEL = 1024
DEPTH = 2
POOL_WIDTH = 512
POOL_WINDOWS = (2, 4, 8, 16)
POOL_GROUP = 128
MAX_POOL_WINDOW = 16
HEAD_DIM = 64
N_Q_HEADS = 8
N_KV_HEADS = 2
GQA_GROUP = 4
ATTN_WIDTH = 512
KV_WIDTH = 128
WINDOW = 128
BLOCK = 128
N_BUCKETS = 32
MAX_DISTANCE = 128
IN_WIDTH = 2304
COL_U, COL_GP, COL_Q, COL_K, COL_V, COL_GA = 0, 512, 1024, 1536, 1664, 1792
DEEPNORM_ALPHA = (2.0 * DEPTH) ** 0.25
LN_EPS = 1e-5
NEG_INF = -1e30

TILE = 512
N_CHUNK = 256
LN_ROWS = 64
QCOLS = GQA_GROUP * BLOCK
VMEM_LIMIT_BYTES = 56 * 1024 * 1024

F32 = jnp.float32
BF16 = jnp.bfloat16


def _silu(v):
    return v / (1.0 + jnp.exp(-v))


def _layer_kernel(x_ref, w_in_ref, pw_ref, pscale_ref, w_out_ref, tab_ref, sink_ref,
                  ln_g_ref, ln_b_ref, o_ref,
                  xb_scr, uext_scr, kext_scr, vext_scr, q_scr, yattn_scr, y_scr):
    s_idx = pl.program_id(1)
    T = TILE
    H = MAX_POOL_WINDOW

    @pl.when(s_idx == 0)
    def _():
        uext_scr[0:H, :] = jnp.zeros((H, POOL_WIDTH), F32)
        kext_scr[:, 0:BLOCK, :] = jnp.zeros((N_KV_HEADS, BLOCK, HEAD_DIM), BF16)
        vext_scr[:, 0:BLOCK] = jnp.zeros((KV_WIDTH, BLOCK), BF16)

    xb_scr[...] = x_ref[...].astype(BF16)

    def proj(col, width):
        return jnp.dot(xb_scr[...], w_in_ref[:, col:col + width], preferred_element_type=F32)

    t_abs = s_idx * T + lax.broadcasted_iota(jnp.int32, (T, POOL_GROUP), 0)
    for p in range(POOL_WIDTH // N_CHUNK):
        c0 = p * N_CHUNK
        uext_scr[H:H + T, c0:c0 + N_CHUNK] = proj(COL_U + c0, N_CHUNK)
        pooled = []
        for gi in range(2 * p, 2 * p + 2):
            w = POOL_WINDOWS[gi]
            g0 = gi * POOL_GROUP
            ext = uext_scr[:, g0:g0 + POOL_GROUP]
            acc, span = ext, 1
            while span < w:
                acc = acc + pltpu.roll(acc, span, axis=0)
                span *= 2
            count = jnp.minimum(t_abs + 1, w).astype(F32)
            pooled.append((acc[H:H + T] / count - ext[H:H + T]).astype(BF16))
        pooled = jnp.concatenate(pooled, axis=1)
        mixed = jnp.dot(pooled, pw_ref[p], preferred_element_type=F32)
        gate = _silu(proj(COL_GP + c0, N_CHUNK))
        y_scr[:, c0:c0 + N_CHUNK] = (mixed * pscale_ref[:, c0:c0 + N_CHUNK] * gate).astype(BF16)
    uext_scr[0:H, :] = uext_scr[T:T + H, :]

    for c in range(ATTN_WIDTH // N_CHUNK):
        q = (proj(COL_Q + c * N_CHUNK, N_CHUNK) * (HEAD_DIM ** -0.5)).astype(BF16)
        for g in range(GQA_GROUP):
            q_scr[c * GQA_GROUP + g] = q[:, g * HEAD_DIM:(g + 1) * HEAD_DIM]
    kv = proj(COL_K, 2 * KV_WIDTH)
    for h in range(N_KV_HEADS):
        kext_scr[h, BLOCK:BLOCK + T, :] = kv[:, h * HEAD_DIM:(h + 1) * HEAD_DIM].astype(BF16)
    vext_scr[:, BLOCK:BLOCK + T] = kv[:, KV_WIDTH:2 * KV_WIDTH].T.astype(BF16)

    first = (s_idx == 0).astype(jnp.int32)
    for h in range(N_KV_HEADS):
        sink = sink_ref[h]
        for i in range(T // BLOCK):
            r0 = i * BLOCK
            qs = q_scr[h * GQA_GROUP:(h + 1) * GQA_GROUP, r0:r0 + BLOCK, :]
            qs = qs.reshape(QCOLS, HEAD_DIM)
            kwin = kext_scr[h, r0:r0 + 2 * BLOCK, :]
            vwin = vext_scr[h * HEAD_DIM:(h + 1) * HEAD_DIM, r0:r0 + 2 * BLOCK]
            sc = lax.dot_general(kwin, qs, (((1,), (1,)), ((), ())),
                                 preferred_element_type=F32)
            sc = sc + tab_ref[first if i == 0 else 0, h]
            m = jnp.maximum(jnp.max(sc, axis=0, keepdims=True), sink)
            e = jnp.exp(sc - m)
            l = jnp.sum(e, axis=0, keepdims=True) + jnp.exp(sink - m)
            o = jnp.dot(vwin, e.astype(BF16), preferred_element_type=F32)
            o = o * (1.0 / l)
            for g in range(GQA_GROUP):
                row = (h * GQA_GROUP + g) * HEAD_DIM
                yattn_scr[row:row + HEAD_DIM, r0:r0 + BLOCK] = o[:, g * BLOCK:(g + 1) * BLOCK]
    kext_scr[:, 0:BLOCK, :] = kext_scr[:, T:T + BLOCK, :]
    vext_scr[:, 0:BLOCK] = vext_scr[:, T:T + BLOCK]

    for c in range(ATTN_WIDTH // N_CHUNK):
        c0 = c * N_CHUNK
        gate = _silu(proj(COL_GA + c0, N_CHUNK))
        y_scr[:, POOL_WIDTH + c0:POOL_WIDTH + c0 + N_CHUNK] = (
            yattn_scr[c0:c0 + N_CHUNK, :].T * gate).astype(BF16)

    o_ref[...] = jnp.dot(y_scr[...], w_out_ref[...], preferred_element_type=F32)
    gamma = ln_g_ref[...]
    beta = ln_b_ref[...]
    for r in range(T // LN_ROWS):
        rows = slice(r * LN_ROWS, (r + 1) * LN_ROWS)
        z = DEEPNORM_ALPHA * x_ref[rows, :] + o_ref[rows, :]
        mu = jnp.mean(z, axis=-1, keepdims=True)
        zc = z - mu
        var = jnp.mean(zc * zc, axis=-1, keepdims=True)
        o_ref[rows, :] = zc * lax.rsqrt(var + LN_EPS) * gamma + beta


def _t5_bucket(dist):
    max_exact = N_BUCKETS // 2
    is_small = dist < max_exact
    large = max_exact + (jnp.log(jnp.maximum(dist, 1).astype(F32) / max_exact)
                         / math.log(MAX_DISTANCE / max_exact) * (N_BUCKETS - max_exact)).astype(jnp.int32)
    large = jnp.minimum(large, N_BUCKETS - 1)
    return jnp.where(is_small, dist, large)


def _bias_tables(rel_bias):
    kj = jnp.arange(2 * BLOCK)[:, None]
    qi = jnp.arange(BLOCK)[None, :]
    dist = qi + BLOCK - kj
    valid = (dist >= 0) & (dist < WINDOW)
    bucket = _t5_bucket(jnp.clip(dist, 0, None))
    onehot = bucket[None, :, :, None] == jnp.arange(N_BUCKETS)
    table = rel_bias.astype(F32).T[:, None, None, :]
    bias = jnp.sum(jnp.where(onehot, table, 0.0), axis=-1)
    tab_any = jnp.where(valid[None], bias, NEG_INF)
    tab_first = jnp.where((valid & (kj >= BLOCK))[None], bias, NEG_INF)
    tab = jnp.stack([tab_any, tab_first])
    tab = tab.reshape(2, N_KV_HEADS, GQA_GROUP, 2 * BLOCK, BLOCK)
    return jnp.transpose(tab, (0, 1, 3, 2, 4)).reshape(2, N_KV_HEADS, 2 * BLOCK, QCOLS)


def _pair_block_diag(pool_w):
    z = jnp.zeros((POOL_GROUP, POOL_GROUP), pool_w.dtype)
    pairs = []
    for p in range(2):
        a, b = pool_w[2 * p], pool_w[2 * p + 1]
        pairs.append(jnp.concatenate([jnp.concatenate([a, z], axis=1),
                                      jnp.concatenate([z, b], axis=1)], axis=0))
    return jnp.stack(pairs)


def _layer(x, ln_g, ln_b, w_in, pool_w, pool_scale, sinks, w_out, tab):
    B, S, D = x.shape
    T = TILE
    const = lambda *shape: pl.BlockSpec(shape, lambda b, s: (0,) * len(shape))
    sink_rows = jnp.repeat(sinks.astype(F32), BLOCK).reshape(N_KV_HEADS, 1, QCOLS)
    return pl.pallas_call(
        _layer_kernel,
        out_shape=jax.ShapeDtypeStruct((B, S, D), F32),
        grid=(B, S // T),
        in_specs=[
            pl.BlockSpec((None, T, D), lambda b, s: (b, s, 0)),
            const(D, IN_WIDTH),
            const(2, N_CHUNK, N_CHUNK),
            const(1, POOL_WIDTH),
            const(D, D),
            const(2, N_KV_HEADS, 2 * BLOCK, QCOLS),
            const(N_KV_HEADS, 1, QCOLS),
            const(1, D),
            const(1, D),
        ],
        out_specs=pl.BlockSpec((None, T, D), lambda b, s: (b, s, 0)),
        scratch_shapes=[
            pltpu.VMEM((T, D), BF16),
            pltpu.VMEM((MAX_POOL_WINDOW + T, POOL_WIDTH), F32),
            pltpu.VMEM((N_KV_HEADS, BLOCK + T, HEAD_DIM), BF16),
            pltpu.VMEM((KV_WIDTH, BLOCK + T), BF16),
            pltpu.VMEM((N_Q_HEADS, T, HEAD_DIM), BF16),
            pltpu.VMEM((ATTN_WIDTH, T), F32),
            pltpu.VMEM((T, D), BF16),
        ],
        compiler_params=pltpu.CompilerParams(
            dimension_semantics=("arbitrary", "arbitrary"),
            vmem_limit_bytes=VMEM_LIMIT_BYTES),
        name="hybrid_layer",
    )(x, w_in.astype(BF16), _pair_block_diag(pool_w).astype(BF16),
      pool_scale.reshape(1, POOL_WIDTH), w_out.astype(BF16), tab, sink_rows,
      ln_g.reshape(1, D), ln_b.reshape(1, D))


@jax.jit
def kernel(x, ln_g, ln_b, w_in, pool_w, pool_scale, sinks, w_out, rel_bias):
    tab = _bias_tables(rel_bias)
    for layer in range(DEPTH):
        x = _layer(x, ln_g[layer], ln_b[layer], w_in[layer], pool_w[layer],
                   pool_scale[layer], sinks[layer], w_out[layer], tab)
    return x
```

```python
import math

import jax
import jax.numpy as jnp
from jax import lax
from jax.experimental import pallas as pl
from jax.experimental.pallas import tpu as pltpu

D_MODEL = 1024
DEPTH = 2
POOL_WIDTH = 512
POOL_WINDOWS = (2, 4, 8, 16)
POOL_GROUP = 128
MAX_POOL_WINDOW = 16
HEAD_DIM = 64
N_Q_HEADS = 8
N_KV_HEADS = 2
GQA_GROUP = 4
ATTN_WIDTH = 512
KV_WIDTH = 128
WINDOW = 128
BLOCK = 128
N_BUCKETS = 32
MAX_DISTANCE = 128
IN_WIDTH = 2304
COL_U, COL_GP, COL_Q, COL_K, COL_V, COL_GA = 0, 512, 1024, 1536, 1664, 1792
DEEPNORM_ALPHA = (2.0 * DEPTH) ** 0.25
LN_EPS = 1e-5
NEG_INF = -1e30

TILE = 512
N_CHUNK = 256
LN_ROWS = 64
QCOLS = GQA_GROUP * BLOCK
VMEM_LIMIT_BYTES = 56 * 1024 * 1024

F32 = jnp.float32
BF16 = jnp.bfloat16


def _silu(v):
    return v / (1.0 + jnp.exp(-v))


def _layer_kernel(x_ref, w_in_ref, pw_ref, pscale_ref, w_out_ref, tab_ref, sink_ref,
                  ln_g_ref, ln_b_ref, o_ref,
                  xb_scr, uext_scr, kext_scr, vext_scr, q_scr, yattn_scr, gate_scr, y_scr):
    s_idx = pl.program_id(1)
    T = TILE
    H = MAX_POOL_WINDOW

    uext_scr[0:H, :] = uext_scr[T:T + H, :]
    kext_scr[:, 0:BLOCK, :] = kext_scr[:, T:T + BLOCK, :]
    vext_scr[:, 0:BLOCK] = vext_scr[:, T:T + BLOCK]

    @pl.when(s_idx == 0)
    def _():
        uext_scr[0:H, :] = jnp.zeros((H, POOL_WIDTH), F32)
        kext_scr[:, 0:BLOCK, :] = jnp.zeros((N_KV_HEADS, BLOCK, HEAD_DIM), BF16)
        vext_scr[:, 0:BLOCK] = jnp.zeros((KV_WIDTH, BLOCK), BF16)

    xb_scr[...] = x_ref[...].astype(BF16)

    def proj(col, width):
        return jnp.dot(xb_scr[...], w_in_ref[:, col:col + width], preferred_element_type=F32)

    def pool_u(p):
        c0 = p * N_CHUNK
        uext_scr[H:H + T, c0:c0 + N_CHUNK] = proj(COL_U + c0, N_CHUNK)

    def pool_mix(p):
        t_abs = s_idx * T + lax.broadcasted_iota(jnp.int32, (T, POOL_GROUP), 0)
        c0 = p * N_CHUNK
        pooled = []
        for gi in range(2 * p, 2 * p + 2):
            w = POOL_WINDOWS[gi]
            g0 = gi * POOL_GROUP
            ext = uext_scr[:, g0:g0 + POOL_GROUP]
            acc, span = ext, 1
            while span < w:
                acc = acc + pltpu.roll(acc, span, axis=0)
                span *= 2
            count = jnp.minimum(t_abs + 1, w).astype(F32)
            pooled.append((acc[H:H + T] / count - ext[H:H + T]).astype(BF16))
        pooled = jnp.concatenate(pooled, axis=1)
        mixed = jnp.dot(pooled, pw_ref[p], preferred_element_type=F32)
        gate = _silu(proj(COL_GP + c0, N_CHUNK))
        y_scr[:, c0:c0 + N_CHUNK] = (mixed * pscale_ref[:, c0:c0 + N_CHUNK] * gate).astype(BF16)

    def q_proj(c):
        q = (proj(COL_Q + c * N_CHUNK, N_CHUNK) * (HEAD_DIM ** -0.5)).astype(BF16)
        for g in range(GQA_GROUP):
            q_scr[c * GQA_GROUP + g] = q[:, g * HEAD_DIM:(g + 1) * HEAD_DIM]

    def kv_proj():
        kv = proj(COL_K, 2 * KV_WIDTH)
        for h in range(N_KV_HEADS):
            kext_scr[h, BLOCK:BLOCK + T, :] = kv[:, h * HEAD_DIM:(h + 1) * HEAD_DIM].astype(BF16)
        vext_scr[:, BLOCK:BLOCK + T] = kv[:, KV_WIDTH:2 * KV_WIDTH].T.astype(BF16)

    def attention(h, i):
        first = (s_idx == 0).astype(jnp.int32)
        sink = sink_ref[h]
        r0 = i * BLOCK
        qs = q_scr[h * GQA_GROUP:(h + 1) * GQA_GROUP, r0:r0 + BLOCK, :]
        qs = qs.reshape(QCOLS, HEAD_DIM)
        kwin = kext_scr[h, r0:r0 + 2 * BLOCK, :]
        vwin = vext_scr[h * HEAD_DIM:(h + 1) * HEAD_DIM, r0:r0 + 2 * BLOCK]
        sc = lax.dot_general(kwin, qs, (((1,), (1,)), ((), ())),
                             preferred_element_type=F32)
        sc = sc + tab_ref[first if i == 0 else 0, h]
        m = jnp.maximum(jnp.max(sc, axis=0, keepdims=True), sink)
        e = jnp.exp(sc - m)
        l = jnp.sum(e, axis=0, keepdims=True) + jnp.exp(sink - m)
        o = jnp.dot(vwin, e.astype(BF16), preferred_element_type=F32)
        o = o * (1.0 / l)
        for g in range(GQA_GROUP):
            row = (h * GQA_GROUP + g) * HEAD_DIM
            yattn_scr[row:row + HEAD_DIM, r0:r0 + BLOCK] = o[:, g * BLOCK:(g + 1) * BLOCK]

    def gate_proj(c):
        c0 = c * N_CHUNK
        gate_scr[:, c0:c0 + N_CHUNK] = _silu(proj(COL_GA + c0, N_CHUNK))

    def gate_attention(c):
        c0 = c * N_CHUNK
        y_scr[:, POOL_WIDTH + c0:POOL_WIDTH + c0 + N_CHUNK] = (
            yattn_scr[c0:c0 + N_CHUNK, :].T * gate_scr[:, c0:c0 + N_CHUNK]).astype(BF16)

    def finish():
        gamma = ln_g_ref[...]
        beta = ln_b_ref[...]
        half = T // 2
        for hf in range(2):
            hrows = slice(hf * half, (hf + 1) * half)
            o_ref[hrows, :] = jnp.dot(y_scr[hrows, :], w_out_ref[...], preferred_element_type=F32)
            for r in range(hf * half // LN_ROWS, (hf + 1) * half // LN_ROWS):
                rows = slice(r * LN_ROWS, (r + 1) * LN_ROWS)
                z = DEEPNORM_ALPHA * x_ref[rows, :] + o_ref[rows, :]
                mu = jnp.mean(z, axis=-1, keepdims=True)
                zc = z - mu
                var = jnp.mean(zc * zc, axis=-1, keepdims=True)
                o_ref[rows, :] = zc * lax.rsqrt(var + LN_EPS) * gamma + beta

    kv_proj()
    q_proj(0)
    attention(0, 0)
    pool_u(0)
    attention(0, 1)
    pool_u(1)
    attention(0, 2)
    q_proj(1)
    attention(0, 3)
    pool_mix(0)
    attention(1, 0)
    gate_proj(0)
    attention(1, 1)
    pool_mix(1)
    attention(1, 2)
    gate_proj(1)
    attention(1, 3)
    gate_attention(0)
    gate_attention(1)
    finish()


def _t5_bucket(dist):
    max_exact = N_BUCKETS // 2
    is_small = dist < max_exact
    large = max_exact + (jnp.log(jnp.maximum(dist, 1).astype(F32) / max_exact)
                         / math.log(MAX_DISTANCE / max_exact) * (N_BUCKETS - max_exact)).astype(jnp.int32)
    large = jnp.minimum(large, N_BUCKETS - 1)
    return jnp.where(is_small, dist, large)


def _bias_tables(rel_bias):
    kj = jnp.arange(2 * BLOCK)[:, None]
    qi = jnp.arange(BLOCK)[None, :]
    dist = qi + BLOCK - kj
    valid = (dist >= 0) & (dist < WINDOW)
    bucket = _t5_bucket(jnp.clip(dist, 0, None))
    onehot = bucket[None, :, :, None] == jnp.arange(N_BUCKETS)
    table = rel_bias.astype(F32).T[:, None, None, :]
    bias = jnp.sum(jnp.where(onehot, table, 0.0), axis=-1)
    tab_any = jnp.where(valid[None], bias, NEG_INF)
    tab_first = jnp.where((valid & (kj >= BLOCK))[None], bias, NEG_INF)
    tab = jnp.stack([tab_any, tab_first])
    tab = tab.reshape(2, N_KV_HEADS, GQA_GROUP, 2 * BLOCK, BLOCK)
    return jnp.transpose(tab, (0, 1, 3, 2, 4)).reshape(2, N_KV_HEADS, 2 * BLOCK, QCOLS)


def _pair_block_diag(pool_w):
    z = jnp.zeros((POOL_GROUP, POOL_GROUP), pool_w.dtype)
    pairs = []
    for p in range(2):
        a, b = pool_w[2 * p], pool_w[2 * p + 1]
        pairs.append(jnp.concatenate([jnp.concatenate([a, z], axis=1),
                                      jnp.concatenate([z, b], axis=1)], axis=0))
    return jnp.stack(pairs)


def _layer(x, ln_g, ln_b, w_in, pool_w, pool_scale, sinks, w_out, tab):
    B, S, D = x.shape
    T = TILE
    const = lambda *shape: pl.BlockSpec(shape, lambda b, s: (0,) * len(shape))
    sink_rows = jnp.repeat(sinks.astype(F32), BLOCK).reshape(N_KV_HEADS, 1, QCOLS)
    return pl.pallas_call(
        _layer_kernel,
        out_shape=jax.ShapeDtypeStruct((B, S, D), F32),
        grid=(B, S // T),
        in_specs=[
            pl.BlockSpec((None, T, D), lambda b, s: (b, s, 0)),
            const(D, IN_WIDTH),
            const(2, N_CHUNK, N_CHUNK),
            const(1, POOL_WIDTH),
            const(D, D),
            const(2, N_KV_HEADS, 2 * BLOCK, QCOLS),
            const(N_KV_HEADS, 1, QCOLS),
            const(1, D),
            const(1, D),
        ],
        out_specs=pl.BlockSpec((None, T, D), lambda b, s: (b, s, 0)),
        scratch_shapes=[
            pltpu.VMEM((T, D), BF16),
            pltpu.VMEM((MAX_POOL_WINDOW + T, POOL_WIDTH), F32),
            pltpu.VMEM((N_KV_HEADS, BLOCK + T, HEAD_DIM), BF16),
            pltpu.VMEM((KV_WIDTH, BLOCK + T), BF16),
            pltpu.VMEM((N_Q_HEADS, T, HEAD_DIM), BF16),
            pltpu.VMEM((ATTN_WIDTH, T), F32),
            pltpu.VMEM((T, ATTN_WIDTH), F32),
            pltpu.VMEM((T, D), BF16),
        ],
        compiler_params=pltpu.CompilerParams(
            dimension_semantics=("arbitrary", "arbitrary"),
            vmem_limit_bytes=VMEM_LIMIT_BYTES),
        name="hybrid_layer",
    )(x, w_in.astype(BF16), _pair_block_diag(pool_w).astype(BF16),
      pool_scale.reshape(1, POOL_WIDTH), w_out.astype(BF16), tab, sink_rows,
      ln_g.reshape(1, D), ln_b.reshape(1, D))


@jax.jit
def kernel(x, ln_g, ln_b, w_in, pool_w, pool_scale, sinks, w_out, rel_bias):
    tab = _bias_tables(rel_bias)
    for layer in range(DEPTH):
        x = _layer(x, ln_g[layer], ln_b[layer], w_in[layer], pool_w[layer],
                   pool_scale[layer], sinks[layer], w_out[layer], tab)
    return x
```

```python
import math

import jax
import jax.numpy as jnp
from jax import lax
from jax.experimental import pallas as pl
from jax.experimental.pallas import tpu as pltpu
import numpy as np

D_MODEL = 1024
DEPTH = 2
POOL_WIDTH = 512
POOL_WINDOWS = (2, 4, 8, 16)
POOL_GROUP = 128
MAX_POOL_WINDOW = 16
HEAD_DIM = 64
N_Q_HEADS = 8
N_KV_HEADS = 2
GQA_GROUP = 4
ATTN_WIDTH = 512
KV_WIDTH = 128
WINDOW = 128
BLOCK = 128
N_BUCKETS = 32
MAX_DISTANCE = 128
IN_WIDTH = 2304
COL_U, COL_GP, COL_Q, COL_K, COL_V, COL_GA = 0, 512, 1024, 1536, 1664, 1792
DEEPNORM_ALPHA = (2.0 * DEPTH) ** 0.25
LN_EPS = 1e-5
NEG_INF = -1e30

TILE = 512
N_CHUNK = 256
LN_ROWS = 64
QCOLS = GQA_GROUP * BLOCK
VMEM_LIMIT_BYTES = 56 * 1024 * 1024

F32 = jnp.float32
BF16 = jnp.bfloat16


def _silu(v):
    return v / (1.0 + jnp.exp(-v))


def _layer_kernel(x_ref, w_in_ref, pw_ref, pscale_ref, w_out_ref, tab_ref, sink_ref,
                  ln_g_ref, ln_b_ref, o_ref,
                  xb_scr, uext_scr, kext_scr, vext_scr, q_scr, yattn_scr, gate_scr, y_scr):
    s_idx = pl.program_id(1)
    T = TILE
    H = MAX_POOL_WINDOW

    uext_scr[0:H, :] = uext_scr[T:T + H, :]
    kext_scr[:, 0:BLOCK, :] = kext_scr[:, T:T + BLOCK, :]
    vext_scr[:, 0:BLOCK] = vext_scr[:, T:T + BLOCK]

    @pl.when(s_idx == 0)
    def _():
        uext_scr[0:H, :] = jnp.zeros((H, POOL_WIDTH), F32)
        kext_scr[:, 0:BLOCK, :] = jnp.zeros((N_KV_HEADS, BLOCK, HEAD_DIM), BF16)
        vext_scr[:, 0:BLOCK] = jnp.zeros((KV_WIDTH, BLOCK), BF16)

    xb_scr[...] = x_ref[...].astype(BF16)

    def proj(col, width):
        return jnp.dot(xb_scr[...], w_in_ref[:, col:col + width], preferred_element_type=F32)

    def pool_u(p):
        c0 = p * N_CHUNK
        uext_scr[H:H + T, c0:c0 + N_CHUNK] = proj(COL_U + c0, N_CHUNK)

    def pool_mix(p):
        t_abs = s_idx * T + lax.broadcasted_iota(jnp.int32, (T, POOL_GROUP), 0)
        c0 = p * N_CHUNK
        pooled = []
        for gi in range(2 * p, 2 * p + 2):
            w = POOL_WINDOWS[gi]
            g0 = gi * POOL_GROUP
            ext = uext_scr[:, g0:g0 + POOL_GROUP]
            acc, span = ext, 1
            while span < w:
                acc = acc + pltpu.roll(acc, span, axis=0)
                span *= 2
            count = jnp.minimum(t_abs + 1, w).astype(F32)
            pooled.append((acc[H:H + T] / count - ext[H:H + T]).astype(BF16))
        pooled = jnp.concatenate(pooled, axis=1)
        mixed = jnp.dot(pooled, pw_ref[p], preferred_element_type=F32)
        gate = _silu(proj(COL_GP + c0, N_CHUNK))
        y_scr[:, c0:c0 + N_CHUNK] = (mixed * pscale_ref[:, c0:c0 + N_CHUNK] * gate).astype(BF16)

    def q_proj(c):
        q = (proj(COL_Q + c * N_CHUNK, N_CHUNK) * (HEAD_DIM ** -0.5)).astype(BF16)
        for g in range(GQA_GROUP):
            q_scr[c * GQA_GROUP + g] = q[:, g * HEAD_DIM:(g + 1) * HEAD_DIM]

    def kv_proj():
        kv = proj(COL_K, 2 * KV_WIDTH)
        for h in range(N_KV_HEADS):
            kext_scr[h, BLOCK:BLOCK + T, :] = kv[:, h * HEAD_DIM:(h + 1) * HEAD_DIM].astype(BF16)
        vext_scr[:, BLOCK:BLOCK + T] = kv[:, KV_WIDTH:2 * KV_WIDTH].T.astype(BF16)

    def attention(h, i):
        first = (s_idx == 0).astype(jnp.int32)
        sink = sink_ref[h]
        r0 = i * BLOCK
        qs = q_scr[h * GQA_GROUP:(h + 1) * GQA_GROUP, r0:r0 + BLOCK, :]
        qs = qs.reshape(QCOLS, HEAD_DIM)
        kwin = kext_scr[h, r0:r0 + 2 * BLOCK, :]
        vwin = vext_scr[h * HEAD_DIM:(h + 1) * HEAD_DIM, r0:r0 + 2 * BLOCK]
        sc = lax.dot_general(kwin, qs, (((1,), (1,)), ((), ())),
                             preferred_element_type=F32)
        sc = sc + tab_ref[first if i == 0 else 0, h]
        m = jnp.maximum(jnp.max(sc, axis=0, keepdims=True), sink)
        e = jnp.exp(sc - m)
        l = jnp.sum(e, axis=0, keepdims=True) + jnp.exp(sink - m)
        o = jnp.dot(vwin, e.astype(BF16), preferred_element_type=F32)
        o = o * (1.0 / l)
        for g in range(GQA_GROUP):
            row = (h * GQA_GROUP + g) * HEAD_DIM
            yattn_scr[row:row + HEAD_DIM, r0:r0 + BLOCK] = o[:, g * BLOCK:(g + 1) * BLOCK]

    def gate_proj(c):
        c0 = c * N_CHUNK
        gate_scr[:, c0:c0 + N_CHUNK] = _silu(proj(COL_GA + c0, N_CHUNK))

    def gate_attention(c):
        c0 = c * N_CHUNK
        y_scr[:, POOL_WIDTH + c0:POOL_WIDTH + c0 + N_CHUNK] = (
            yattn_scr[c0:c0 + N_CHUNK, :].T * gate_scr[:, c0:c0 + N_CHUNK]).astype(BF16)

    def finish():
        gamma = ln_g_ref[...]
        beta = ln_b_ref[...]
        half = T // 2
        for hf in range(2):
            hrows = slice(hf * half, (hf + 1) * half)
            o_ref[hrows, :] = jnp.dot(y_scr[hrows, :], w_out_ref[...], preferred_element_type=F32)
            for r in range(hf * half // LN_ROWS, (hf + 1) * half // LN_ROWS):
                rows = slice(r * LN_ROWS, (r + 1) * LN_ROWS)
                z = DEEPNORM_ALPHA * x_ref[rows, :] + o_ref[rows, :]
                mu = jnp.mean(z, axis=-1, keepdims=True)
                zc = z - mu
                var = jnp.mean(zc * zc, axis=-1, keepdims=True)
                o_ref[rows, :] = zc * lax.rsqrt(var + LN_EPS) * gamma + beta

    kv_proj()
    q_proj(0)
    attention(0, 0)
    pool_u(0)
    attention(0, 1)
    pool_u(1)
    attention(0, 2)
    q_proj(1)
    attention(0, 3)
    pool_mix(0)
    attention(1, 0)
    gate_proj(0)
    attention(1, 1)
    pool_mix(1)
    attention(1, 2)
    gate_proj(1)
    attention(1, 3)
    gate_attention(0)
    gate_attention(1)
    finish()


def _t5_bucket(dist):
    max_exact = N_BUCKETS // 2
    ratio = (np.log(np.maximum(dist, 1).astype(np.float32) / np.float32(max_exact))
             / np.float32(math.log(MAX_DISTANCE / max_exact)))
    large = max_exact + (ratio * np.float32(N_BUCKETS - max_exact)).astype(np.int32)
    large = np.minimum(large, N_BUCKETS - 1)
    return np.where(dist < max_exact, dist, large)


def _bias_tables(rel_bias):
    kj = np.arange(2 * BLOCK)[:, None]
    qi = np.arange(BLOCK)[None, :]
    dist = qi + BLOCK - kj
    valid = (dist >= 0) & (dist < WINDOW)
    bucket = _t5_bucket(np.clip(dist, 0, None))
    onehot = bucket[None, :, :, None] == np.arange(N_BUCKETS)
    table = rel_bias.astype(F32).T[:, None, None, :]
    bias = jnp.sum(jnp.where(onehot, table, 0.0), axis=-1)
    tab_any = jnp.where(valid[None], bias, NEG_INF)
    tab_first = jnp.where((valid & (kj >= BLOCK))[None], bias, NEG_INF)
    tab = jnp.stack([tab_any, tab_first])
    tab = tab.reshape(2, N_KV_HEADS, GQA_GROUP, 2 * BLOCK, BLOCK)
    return jnp.transpose(tab, (0, 1, 3, 2, 4)).reshape(2, N_KV_HEADS, 2 * BLOCK, QCOLS)


def _pair_block_diag(pool_w):
    z = jnp.zeros((POOL_GROUP, POOL_GROUP), pool_w.dtype)
    pairs = []
    for p in range(2):
        a, b = pool_w[2 * p], pool_w[2 * p + 1]
        pairs.append(jnp.concatenate([jnp.concatenate([a, z], axis=1),
                                      jnp.concatenate([z, b], axis=1)], axis=0))
    return jnp.stack(pairs)


def _layer(x, ln_g, ln_b, w_in, pool_w, pool_scale, sinks, w_out, tab):
    B, S, D = x.shape
    T = TILE
    const = lambda *shape: pl.BlockSpec(shape, lambda b, s: (0,) * len(shape))
    sink_rows = jnp.repeat(sinks.astype(F32), BLOCK).reshape(N_KV_HEADS, 1, QCOLS)
    return pl.pallas_call(
        _layer_kernel,
        out_shape=jax.ShapeDtypeStruct((B, S, D), F32),
        grid=(B, S // T),
        in_specs=[
            pl.BlockSpec((None, T, D), lambda b, s: (b, s, 0)),
            const(D, IN_WIDTH),
            const(2, N_CHUNK, N_CHUNK),
            const(1, POOL_WIDTH),
            const(D, D),
            const(2, N_KV_HEADS, 2 * BLOCK, QCOLS),
            const(N_KV_HEADS, 1, QCOLS),
            const(1, D),
            const(1, D),
        ],
        out_specs=pl.BlockSpec((None, T, D), lambda b, s: (b, s, 0)),
        scratch_shapes=[
            pltpu.VMEM((T, D), BF16),
            pltpu.VMEM((MAX_POOL_WINDOW + T, POOL_WIDTH), F32),
            pltpu.VMEM((N_KV_HEADS, BLOCK + T, HEAD_DIM), BF16),
            pltpu.VMEM((KV_WIDTH, BLOCK + T), BF16),
            pltpu.VMEM((N_Q_HEADS, T, HEAD_DIM), BF16),
            pltpu.VMEM((ATTN_WIDTH, T), F32),
            pltpu.VMEM((T, ATTN_WIDTH), F32),
            pltpu.VMEM((T, D), BF16),
        ],
        compiler_params=pltpu.CompilerParams(
            dimension_semantics=("arbitrary", "arbitrary"),
            vmem_limit_bytes=VMEM_LIMIT_BYTES),
        name="hybrid_layer",
    )(x, w_in.astype(BF16), _pair_block_diag(pool_w).astype(BF16),
      pool_scale.reshape(1, POOL_WIDTH), w_out.astype(BF16), tab, sink_rows,
      ln_g.reshape(1, D), ln_b.reshape(1, D))


@jax.jit
def kernel(x, ln_g, ln_b, w_in, pool_w, pool_scale, sinks, w_out, rel_bias):
    tab = _bias_tables(rel_bias)
    for layer in range(DEPTH):
        x = _layer(x, ln_g[layer], ln_b[layer], w_in[layer], pool_w[layer],
                   pool_scale[layer], sinks[layer], w_out[layer], tab)
    return x
```

```python
import math

import jax
import jax.numpy as jnp
from jax import lax
from jax.experimental import pallas as pl
from jax.experimental.pallas import tpu as pltpu
import numpy as np

D_MODEL = 1024
DEPTH = 2
POOL_WIDTH = 512
POOL_WINDOWS = (2, 4, 8, 16)
POOL_GROUP = 128
MAX_POOL_WINDOW = 16
HEAD_DIM = 64
N_Q_HEADS = 8
N_KV_HEADS = 2
GQA_GROUP = 4
ATTN_WIDTH = 512
KV_WIDTH = 128
WINDOW = 128
BLOCK = 128
N_BUCKETS = 32
MAX_DISTANCE = 128
IN_WIDTH = 2304
COL_U, COL_GP, COL_Q, COL_K, COL_V, COL_GA = 0, 512, 1024, 1536, 1664, 1792
DEEPNORM_ALPHA = (2.0 * DEPTH) ** 0.25
LN_EPS = 1e-5
NEG_INF = -1e30

TILE = 1024
N_CHUNK = 256
LN_ROWS = 64
QCOLS = GQA_GROUP * BLOCK
VMEM_LIMIT_BYTES = 56 * 1024 * 1024

F32 = jnp.float32
BF16 = jnp.bfloat16


def _silu(v):
    return v / (1.0 + jnp.exp(-v))


def _layer_kernel(x_ref, w_in_ref, pw_ref, pscale_ref, w_out_ref, tab_ref, sink_ref,
                  ln_g_ref, ln_b_ref, o_ref,
                  xb_scr, uext_scr, kext_scr, vext_scr, q_scr, yattn_scr, gate_scr, y_scr):
    s_idx = pl.program_id(1)
    T = TILE
    H = MAX_POOL_WINDOW

    uext_scr[0:H, :] = uext_scr[T:T + H, :]
    kext_scr[:, 0:BLOCK, :] = kext_scr[:, T:T + BLOCK, :]
    vext_scr[:, 0:BLOCK] = vext_scr[:, T:T + BLOCK]

    @pl.when(s_idx == 0)
    def _():
        uext_scr[0:H, :] = jnp.zeros((H, POOL_WIDTH), F32)
        kext_scr[:, 0:BLOCK, :] = jnp.zeros((N_KV_HEADS, BLOCK, HEAD_DIM), BF16)
        vext_scr[:, 0:BLOCK] = jnp.zeros((KV_WIDTH, BLOCK), BF16)

    xb_scr[...] = x_ref[...].astype(BF16)

    def proj(col, width):
        return jnp.dot(xb_scr[...], w_in_ref[:, col:col + width], preferred_element_type=F32)

    def pool_u(p):
        c0 = p * N_CHUNK
        uext_scr[H:H + T, c0:c0 + N_CHUNK] = proj(COL_U + c0, N_CHUNK)

    def pool_mix(p):
        t_abs = s_idx * T + lax.broadcasted_iota(jnp.int32, (T, POOL_GROUP), 0)
        c0 = p * N_CHUNK
        pooled = []
        for gi in range(2 * p, 2 * p + 2):
            w = POOL_WINDOWS[gi]
            g0 = gi * POOL_GROUP
            ext = uext_scr[:, g0:g0 + POOL_GROUP]
            acc, span = ext, 1
            while span < w:
                acc = acc + pltpu.roll(acc, span, axis=0)
                span *= 2
            count = jnp.minimum(t_abs + 1, w).astype(F32)
            pooled.append((acc[H:H + T] / count - ext[H:H + T]).astype(BF16))
        pooled = jnp.concatenate(pooled, axis=1)
        mixed = jnp.dot(pooled, pw_ref[p], preferred_element_type=F32)
        gate = _silu(proj(COL_GP + c0, N_CHUNK))
        y_scr[:, c0:c0 + N_CHUNK] = (mixed * pscale_ref[:, c0:c0 + N_CHUNK] * gate).astype(BF16)

    def q_proj(c):
        q = (proj(COL_Q + c * N_CHUNK, N_CHUNK) * (HEAD_DIM ** -0.5)).astype(BF16)
        for g in range(GQA_GROUP):
            q_scr[c * GQA_GROUP + g] = q[:, g * HEAD_DIM:(g + 1) * HEAD_DIM]

    def kv_proj():
        kv = proj(COL_K, 2 * KV_WIDTH)
        for h in range(N_KV_HEADS):
            kext_scr[h, BLOCK:BLOCK + T, :] = kv[:, h * HEAD_DIM:(h + 1) * HEAD_DIM].astype(BF16)
        vext_scr[:, BLOCK:BLOCK + T] = kv[:, KV_WIDTH:2 * KV_WIDTH].T.astype(BF16)

    def attention(h, i):
        first = (s_idx == 0).astype(jnp.int32)
        sink = sink_ref[h]
        r0 = i * BLOCK
        qs = q_scr[h * GQA_GROUP:(h + 1) * GQA_GROUP, r0:r0 + BLOCK, :]
        qs = qs.reshape(QCOLS, HEAD_DIM)
        kwin = kext_scr[h, r0:r0 + 2 * BLOCK, :]
        vwin = vext_scr[h * HEAD_DIM:(h + 1) * HEAD_DIM, r0:r0 + 2 * BLOCK]
        sc = lax.dot_general(kwin, qs, (((1,), (1,)), ((), ())),
                             preferred_element_type=F32)
        sc = sc + tab_ref[first if i == 0 else 0, h]
        m = jnp.maximum(jnp.max(sc, axis=0, keepdims=True), sink)
        e = jnp.exp(sc - m)
        l = jnp.sum(e, axis=0, keepdims=True) + jnp.exp(sink - m)
        o = jnp.dot(vwin, e.astype(BF16), preferred_element_type=F32)
        o = o * (1.0 / l)
        for g in range(GQA_GROUP):
            row = (h * GQA_GROUP + g) * HEAD_DIM
            yattn_scr[row:row + HEAD_DIM, r0:r0 + BLOCK] = o[:, g * BLOCK:(g + 1) * BLOCK]

    def gate_proj(c):
        c0 = c * N_CHUNK
        gate_scr[:, c0:c0 + N_CHUNK] = _silu(proj(COL_GA + c0, N_CHUNK))

    def gate_attention(c):
        c0 = c * N_CHUNK
        y_scr[:, POOL_WIDTH + c0:POOL_WIDTH + c0 + N_CHUNK] = (
            yattn_scr[c0:c0 + N_CHUNK, :].T * gate_scr[:, c0:c0 + N_CHUNK]).astype(BF16)

    def finish():
        gamma = ln_g_ref[...]
        beta = ln_b_ref[...]
        half = T // 2
        for hf in range(2):
            hrows = slice(hf * half, (hf + 1) * half)
            o_ref[hrows, :] = jnp.dot(y_scr[hrows, :], w_out_ref[...], preferred_element_type=F32)
            for r in range(hf * half // LN_ROWS, (hf + 1) * half // LN_ROWS):
                rows = slice(r * LN_ROWS, (r + 1) * LN_ROWS)
                z = DEEPNORM_ALPHA * x_ref[rows, :] + o_ref[rows, :]
                mu = jnp.mean(z, axis=-1, keepdims=True)
                zc = z - mu
                var = jnp.mean(zc * zc, axis=-1, keepdims=True)
                o_ref[rows, :] = zc * lax.rsqrt(var + LN_EPS) * gamma + beta

    fillers = [lambda: pool_u(0), lambda: pool_u(1), lambda: gate_proj(0), lambda: gate_proj(1),
               lambda: pool_mix(0), lambda: pool_mix(1), lambda: gate_attention(0), lambda: None]
    blocks = [(h, i) for h in range(N_KV_HEADS) for i in range(T // BLOCK)]
    kv_proj()
    q_proj(0)
    q_proj(1)
    for n, filler in enumerate(fillers):
        for h, i in blocks[2 * n:2 * n + 2]:
            attention(h, i)
        filler()
    gate_attention(1)
    finish()


def _t5_bucket(dist):
    max_exact = N_BUCKETS // 2
    ratio = (np.log(np.maximum(dist, 1).astype(np.float32) / np.float32(max_exact))
             / np.float32(math.log(MAX_DISTANCE / max_exact)))
    large = max_exact + (ratio * np.float32(N_BUCKETS - max_exact)).astype(np.int32)
    large = np.minimum(large, N_BUCKETS - 1)
    return np.where(dist < max_exact, dist, large)


def _bias_tables(rel_bias):
    kj = np.arange(2 * BLOCK)[:, None]
    qi = np.arange(BLOCK)[None, :]
    dist = qi + BLOCK - kj
    valid = (dist >= 0) & (dist < WINDOW)
    bucket = _t5_bucket(np.clip(dist, 0, None))
    onehot = bucket[None, :, :, None] == np.arange(N_BUCKETS)
    table = rel_bias.astype(F32).T[:, None, None, :]
    bias = jnp.sum(jnp.where(onehot, table, 0.0), axis=-1)
    tab_any = jnp.where(valid[None], bias, NEG_INF)
    tab_first = jnp.where((valid & (kj >= BLOCK))[None], bias, NEG_INF)
    tab = jnp.stack([tab_any, tab_first])
    tab = tab.reshape(2, N_KV_HEADS, GQA_GROUP, 2 * BLOCK, BLOCK)
    return jnp.transpose(tab, (0, 1, 3, 2, 4)).reshape(2, N_KV_HEADS, 2 * BLOCK, QCOLS)


def _pair_block_diag(pool_w):
    z = jnp.zeros((POOL_GROUP, POOL_GROUP), pool_w.dtype)
    pairs = []
    for p in range(2):
        a, b = pool_w[2 * p], pool_w[2 * p + 1]
        pairs.append(jnp.concatenate([jnp.concatenate([a, z], axis=1),
                                      jnp.concatenate([z, b], axis=1)], axis=0))
    return jnp.stack(pairs)


def _layer(x, ln_g, ln_b, w_in, pool_w, pool_scale, sinks, w_out, tab):
    B, S, D = x.shape
    T = TILE
    const = lambda *shape: pl.BlockSpec(shape, lambda b, s: (0,) * len(shape))
    sink_rows = jnp.repeat(sinks.astype(F32), BLOCK).reshape(N_KV_HEADS, 1, QCOLS)
    return pl.pallas_call(
        _layer_kernel,
        out_shape=jax.ShapeDtypeStruct((B, S, D), F32),
        grid=(B, S // T),
        in_specs=[
            pl.BlockSpec((None, T, D), lambda b, s: (b, s, 0)),
            const(D, IN_WIDTH),
            const(2, N_CHUNK, N_CHUNK),
            const(1, POOL_WIDTH),
            const(D, D),
            const(2, N_KV_HEADS, 2 * BLOCK, QCOLS),
            const(N_KV_HEADS, 1, QCOLS),
            const(1, D),
            const(1, D),
        ],
        out_specs=pl.BlockSpec((None, T, D), lambda b, s: (b, s, 0)),
        scratch_shapes=[
            pltpu.VMEM((T, D), BF16),
            pltpu.VMEM((MAX_POOL_WINDOW + T, POOL_WIDTH), F32),
            pltpu.VMEM((N_KV_HEADS, BLOCK + T, HEAD_DIM), BF16),
            pltpu.VMEM((KV_WIDTH, BLOCK + T), BF16),
            pltpu.VMEM((N_Q_HEADS, T, HEAD_DIM), BF16),
            pltpu.VMEM((ATTN_WIDTH, T), F32),
            pltpu.VMEM((T, ATTN_WIDTH), F32),
            pltpu.VMEM((T, D), BF16),
        ],
        compiler_params=pltpu.CompilerParams(
            dimension_semantics=("arbitrary", "arbitrary"),
            vmem_limit_bytes=VMEM_LIMIT_BYTES),
        name="hybrid_layer",
    )(x, w_in.astype(BF16), _pair_block_diag(pool_w).astype(BF16),
      pool_scale.reshape(1, POOL_WIDTH), w_out.astype(BF16), tab, sink_rows,
      ln_g.reshape(1, D), ln_b.reshape(1, D))


@jax.jit
def kernel(x, ln_g, ln_b, w_in, pool_w, pool_scale, sinks, w_out, rel_bias):
    tab = _bias_tables(rel_bias)
    for layer in range(DEPTH):
        x = _layer(x, ln_g[layer], ln_b[layer], w_in[layer], pool_w[layer],
                   pool_scale[layer], sinks[layer], w_out[layer], tab)
    return x
```

```python
import math

import jax
import jax.numpy as jnp
from jax import lax
from jax.experimental import pallas as pl
from jax.experimental.pallas import tpu as pltpu
import numpy as np

D_MODEL = 1024
DEPTH = 2
POOL_WIDTH = 512
POOL_WINDOWS = (2, 4, 8, 16)
POOL_GROUP = 128
MAX_POOL_WINDOW = 16
HEAD_DIM = 64
N_Q_HEADS = 8
N_KV_HEADS = 2
GQA_GROUP = 4
ATTN_WIDTH = 512
KV_WIDTH = 128
WINDOW = 128
BLOCK = 128
N_BUCKETS = 32
MAX_DISTANCE = 128
IN_WIDTH = 2304
COL_U, COL_GP, COL_Q, COL_K, COL_V, COL_GA = 0, 512, 1024, 1536, 1664, 1792
DEEPNORM_ALPHA = (2.0 * DEPTH) ** 0.25
LN_EPS = 1e-5
NEG_INF = -1e30
LOG2E = math.log2(math.e)

TILE = 1024
N_CHUNK = 256
LN_ROWS = 64
OUT_SPLIT = 4
QCOLS = GQA_GROUP * BLOCK
VMEM_LIMIT_BYTES = 56 * 1024 * 1024

F32 = jnp.float32
BF16 = jnp.bfloat16


def _silu(v):
    return v / (1.0 + jnp.exp2(v * (-LOG2E)))


def _layer_kernel(x_ref, w_in_ref, pw_ref, pscale_ref, w_out_ref, tab_ref, sink_ref,
                  ln_g_ref, ln_b_ref, o_ref,
                  xb_scr, uext_scr, kext_scr, vext_scr, q_scr, yattn_scr, gate_scr, y_scr):
    s_idx = pl.program_id(1)
    T = TILE
    H = MAX_POOL_WINDOW

    uext_scr[0:H, :] = uext_scr[T:T + H, :]
    kext_scr[:, 0:BLOCK, :] = kext_scr[:, T:T + BLOCK, :]
    vext_scr[:, 0:BLOCK] = vext_scr[:, T:T + BLOCK]

    @pl.when(s_idx == 0)
    def _():
        uext_scr[0:H, :] = jnp.zeros((H, POOL_WIDTH), F32)
        kext_scr[:, 0:BLOCK, :] = jnp.zeros((N_KV_HEADS, BLOCK, HEAD_DIM), BF16)
        vext_scr[:, 0:BLOCK] = jnp.zeros((KV_WIDTH, BLOCK), BF16)

    xb_scr[...] = x_ref[...].astype(BF16)

    def proj(col, width):
        return jnp.dot(xb_scr[...], w_in_ref[:, col:col + width], preferred_element_type=F32)

    def pool_u(p):
        c0 = p * N_CHUNK
        uext_scr[H:H + T, c0:c0 + N_CHUNK] = proj(COL_U + c0, N_CHUNK)

    def pool_mix(p):
        t_top = s_idx * T + lax.broadcasted_iota(jnp.int32, (H, POOL_GROUP), 0)
        c0 = p * N_CHUNK
        pooled = []
        for gi in range(2 * p, 2 * p + 2):
            w = POOL_WINDOWS[gi]
            g0 = gi * POOL_GROUP
            ext = uext_scr[:, g0:g0 + POOL_GROUP]
            acc, span = ext, 1
            while span < w:
                acc = acc + pltpu.roll(acc, span, axis=0)
                span *= 2
            top = acc[H:2 * H] / jnp.minimum(t_top + 1, w).astype(F32)
            mean = jnp.concatenate([top, acc[2 * H:H + T] * (1.0 / w)], axis=0)
            pooled.append((mean - ext[H:H + T]).astype(BF16))
        pooled = jnp.concatenate(pooled, axis=1)
        mixed = jnp.dot(pooled, pw_ref[p], preferred_element_type=F32)
        gate = _silu(proj(COL_GP + c0, N_CHUNK))
        y_scr[:, c0:c0 + N_CHUNK] = (mixed * pscale_ref[:, c0:c0 + N_CHUNK] * gate).astype(BF16)

    def q_proj(c):
        q = proj(COL_Q + c * N_CHUNK, N_CHUNK).astype(BF16)
        for g in range(GQA_GROUP):
            q_scr[c * GQA_GROUP + g] = q[:, g * HEAD_DIM:(g + 1) * HEAD_DIM]

    def kv_proj():
        kv = proj(COL_K, 2 * KV_WIDTH)
        for h in range(N_KV_HEADS):
            kext_scr[h, BLOCK:BLOCK + T, :] = kv[:, h * HEAD_DIM:(h + 1) * HEAD_DIM].astype(BF16)
        vext_scr[:, BLOCK:BLOCK + T] = kv[:, KV_WIDTH:2 * KV_WIDTH].T.astype(BF16)

    def attention(h, i):
        first = (s_idx == 0).astype(jnp.int32)
        sink = sink_ref[h]
        r0 = i * BLOCK
        qs = q_scr[h * GQA_GROUP:(h + 1) * GQA_GROUP, r0:r0 + BLOCK, :]
        qs = qs.reshape(QCOLS, HEAD_DIM)
        kwin = kext_scr[h, r0:r0 + 2 * BLOCK, :]
        vwin = vext_scr[h * HEAD_DIM:(h + 1) * HEAD_DIM, r0:r0 + 2 * BLOCK]
        sc = lax.dot_general(kwin, qs, (((1,), (1,)), ((), ())),
                             preferred_element_type=F32)
        sc = sc + tab_ref[first if i == 0 else 0, h]
        m = jnp.maximum(jnp.max(sc, axis=0, keepdims=True), sink)
        e = jnp.exp(sc - m)
        l = jnp.sum(e, axis=0, keepdims=True) + jnp.exp(sink - m)
        o = jnp.dot(vwin, e.astype(BF16), preferred_element_type=F32)
        o = o * (1.0 / l)
        for g in range(GQA_GROUP):
            row = (h * GQA_GROUP + g) * HEAD_DIM
            yattn_scr[row:row + HEAD_DIM, r0:r0 + BLOCK] = o[:, g * BLOCK:(g + 1) * BLOCK]

    def gate_proj(c):
        c0 = c * N_CHUNK
        gate_scr[:, c0:c0 + N_CHUNK] = _silu(proj(COL_GA + c0, N_CHUNK))

    def gate_attention(c):
        c0 = c * N_CHUNK
        y_scr[:, POOL_WIDTH + c0:POOL_WIDTH + c0 + N_CHUNK] = (
            yattn_scr[c0:c0 + N_CHUNK, :].T * gate_scr[:, c0:c0 + N_CHUNK]).astype(BF16)

    def finish():
        gamma = ln_g_ref[...]
        beta = ln_b_ref[...]
        half = T // OUT_SPLIT
        for hf in range(OUT_SPLIT):
            hrows = slice(hf * half, (hf + 1) * half)
            o_ref[hrows, :] = jnp.dot(y_scr[hrows, :], w_out_ref[...], preferred_element_type=F32)
            for r in range(hf * half // LN_ROWS, (hf + 1) * half // LN_ROWS):
                rows = slice(r * LN_ROWS, (r + 1) * LN_ROWS)
                z = DEEPNORM_ALPHA * x_ref[rows, :] + o_ref[rows, :]
                mu = jnp.mean(z, axis=-1, keepdims=True)
                zc = z - mu
                var = jnp.mean(zc * zc, axis=-1, keepdims=True)
                o_ref[rows, :] = zc * lax.rsqrt(var + LN_EPS) * gamma + beta

    fillers = [lambda: pool_u(0), lambda: pool_u(1), lambda: gate_proj(0), lambda: gate_proj(1),
               lambda: pool_mix(0), lambda: pool_mix(1), lambda: gate_attention(0), lambda: None]
    blocks = [(h, i) for h in range(N_KV_HEADS) for i in range(T // BLOCK)]
    kv_proj()
    q_proj(0)
    q_proj(1)
    for n, filler in enumerate(fillers):
        for h, i in blocks[2 * n:2 * n + 2]:
            attention(h, i)
        filler()
    gate_attention(1)
    finish()


def _t5_bucket(dist):
    max_exact = N_BUCKETS // 2
    ratio = (np.log(np.maximum(dist, 1).astype(np.float32) / np.float32(max_exact))
             / np.float32(math.log(MAX_DISTANCE / max_exact)))
    large = max_exact + (ratio * np.float32(N_BUCKETS - max_exact)).astype(np.int32)
    large = np.minimum(large, N_BUCKETS - 1)
    return np.where(dist < max_exact, dist, large)


def _bias_tables(rel_bias):
    kj = np.arange(2 * BLOCK)[:, None]
    qi = np.arange(BLOCK)[None, :]
    dist = qi + BLOCK - kj
    valid = (dist >= 0) & (dist < WINDOW)
    bucket = _t5_bucket(np.clip(dist, 0, None))
    onehot = bucket[None, :, :, None] == np.arange(N_BUCKETS)
    table = rel_bias.astype(F32).T[:, None, None, :]
    bias = jnp.sum(jnp.where(onehot, table, 0.0), axis=-1)
    tab_any = jnp.where(valid[None], bias, NEG_INF)
    tab_first = jnp.where((valid & (kj >= BLOCK))[None], bias, NEG_INF)
    tab = jnp.stack([tab_any, tab_first])
    tab = tab.reshape(2, N_KV_HEADS, GQA_GROUP, 2 * BLOCK, BLOCK)
    return jnp.transpose(tab, (0, 1, 3, 2, 4)).reshape(2, N_KV_HEADS, 2 * BLOCK, QCOLS)


def _query_scale():
    scale = np.ones((1, IN_WIDTH), np.float32)
    scale[:, COL_Q:COL_K] = HEAD_DIM ** -0.5
    return scale


def _pair_block_diag(pool_w):
    z = jnp.zeros((POOL_GROUP, POOL_GROUP), pool_w.dtype)
    pairs = []
    for p in range(2):
        a, b = pool_w[2 * p], pool_w[2 * p + 1]
        pairs.append(jnp.concatenate([jnp.concatenate([a, z], axis=1),
                                      jnp.concatenate([z, b], axis=1)], axis=0))
    return jnp.stack(pairs)


def _layer(x, ln_g, ln_b, w_in, pool_w, pool_scale, sinks, w_out, tab):
    B, S, D = x.shape
    T = TILE
    const = lambda *shape: pl.BlockSpec(shape, lambda b, s: (0,) * len(shape))
    sink_rows = jnp.repeat(sinks.astype(F32), BLOCK).reshape(N_KV_HEADS, 1, QCOLS)
    return pl.pallas_call(
        _layer_kernel,
        out_shape=jax.ShapeDtypeStruct((B, S, D), F32),
        grid=(B, S // T),
        in_specs=[
            pl.BlockSpec((None, T, D), lambda b, s: (b, s, 0)),
            const(D, IN_WIDTH),
            const(2, N_CHUNK, N_CHUNK),
            const(1, POOL_WIDTH),
            const(D, D),
            const(2, N_KV_HEADS, 2 * BLOCK, QCOLS),
            const(N_KV_HEADS, 1, QCOLS),
            const(1, D),
            const(1, D),
        ],
        out_specs=pl.BlockSpec((None, T, D), lambda b, s: (b, s, 0)),
        scratch_shapes=[
            pltpu.VMEM((T, D), BF16),
            pltpu.VMEM((MAX_POOL_WINDOW + T, POOL_WIDTH), F32),
            pltpu.VMEM((N_KV_HEADS, BLOCK + T, HEAD_DIM), BF16),
            pltpu.VMEM((KV_WIDTH, BLOCK + T), BF16),
            pltpu.VMEM((N_Q_HEADS, T, HEAD_DIM), BF16),
            pltpu.VMEM((ATTN_WIDTH, T), F32),
            pltpu.VMEM((T, ATTN_WIDTH), F32),
            pltpu.VMEM((T, D), BF16),
        ],
        compiler_params=pltpu.CompilerParams(
            dimension_semantics=("arbitrary", "arbitrary"),
            vmem_limit_bytes=VMEM_LIMIT_BYTES),
        name="hybrid_layer",
    )(x, (w_in * _query_scale()).astype(BF16), _pair_block_diag(pool_w).astype(BF16),
      pool_scale.reshape(1, POOL_WIDTH), w_out.astype(BF16), tab, sink_rows,
      ln_g.reshape(1, D), ln_b.reshape(1, D))


@jax.jit
def kernel(x, ln_g, ln_b, w_in, pool_w, pool_scale, sinks, w_out, rel_bias):
    tab = _bias_tables(rel_bias)
    for layer in range(DEPTH):
        x = _layer(x, ln_g[layer], ln_b[layer], w_in[layer], pool_w[layer],
                   pool_scale[layer], sinks[layer], w_out[layer], tab)
    return x
```

```python
import math

import jax
import jax.numpy as jnp
from jax import lax
from jax.experimental import pallas as pl
from jax.experimental.pallas import tpu as pltpu
import numpy as np

D_MODEL = 1024
DEPTH = 2
POOL_WIDTH = 512
POOL_WINDOWS = (2, 4, 8, 16)
POOL_GROUP = 128
MAX_POOL_WINDOW = 16
HEAD_DIM = 64
N_Q_HEADS = 8
N_KV_HEADS = 2
GQA_GROUP = 4
ATTN_WIDTH = 512
KV_WIDTH = 128
WINDOW = 128
BLOCK = 128
N_BUCKETS = 32
MAX_DISTANCE = 128
IN_WIDTH = 2304
COL_U, COL_GP, COL_Q, COL_K, COL_V, COL_GA = 0, 512, 1024, 1536, 1664, 1792
DEEPNORM_ALPHA = (2.0 * DEPTH) ** 0.25
LN_EPS = 1e-5
NEG_INF = -1e30
LOG2E = math.log2(math.e)

TILE = 1024
N_CHUNK = 256
LN_ROWS = 64
LOOKAHEAD = 2
OUT_SPLIT = 4
QCOLS = GQA_GROUP * BLOCK
VMEM_LIMIT_BYTES = 56 * 1024 * 1024

F32 = jnp.float32
BF16 = jnp.bfloat16


def _silu(v):
    return v / (1.0 + jnp.exp2(v * (-LOG2E)))


def _layer_kernel(x_ref, w_in_ref, pw_ref, pscale_ref, w_out_ref, tab_ref, sink_ref,
                  ln_g_ref, ln_b_ref, o_ref,
                  xb_scr, uext_scr, kext_scr, vext_scr, q_scr, yattn_scr, gate_scr, y_scr):
    s_idx = pl.program_id(1)
    T = TILE
    H = MAX_POOL_WINDOW

    uext_scr[0:H, :] = uext_scr[T:T + H, :]
    kext_scr[:, 0:BLOCK, :] = kext_scr[:, T:T + BLOCK, :]
    vext_scr[:, 0:BLOCK] = vext_scr[:, T:T + BLOCK]

    @pl.when(s_idx == 0)
    def _():
        uext_scr[0:H, :] = jnp.zeros((H, POOL_WIDTH), F32)
        kext_scr[:, 0:BLOCK, :] = jnp.zeros((N_KV_HEADS, BLOCK, HEAD_DIM), BF16)
        vext_scr[:, 0:BLOCK] = jnp.zeros((KV_WIDTH, BLOCK), BF16)

    xb_scr[...] = x_ref[...].astype(BF16)

    def proj(col, width):
        return jnp.dot(xb_scr[...], w_in_ref[:, col:col + width], preferred_element_type=F32)

    def pool_u(p):
        c0 = p * N_CHUNK
        uext_scr[H:H + T, c0:c0 + N_CHUNK] = proj(COL_U + c0, N_CHUNK)

    def pool_mix(p):
        t_top = s_idx * T + lax.broadcasted_iota(jnp.int32, (H, POOL_GROUP), 0)
        c0 = p * N_CHUNK
        pooled = []
        for gi in range(2 * p, 2 * p + 2):
            w = POOL_WINDOWS[gi]
            g0 = gi * POOL_GROUP
            ext = uext_scr[:, g0:g0 + POOL_GROUP]
            acc, span = ext, 1
            while span < w:
                acc = acc + pltpu.roll(acc, span, axis=0)
                span *= 2
            top = acc[H:2 * H] / jnp.minimum(t_top + 1, w).astype(F32)
            mean = jnp.concatenate([top, acc[2 * H:H + T] * (1.0 / w)], axis=0)
            pooled.append((mean - ext[H:H + T]).astype(BF16))
        pooled = jnp.concatenate(pooled, axis=1)
        mixed = jnp.dot(pooled, pw_ref[p], preferred_element_type=F32)
        gate = _silu(proj(COL_GP + c0, N_CHUNK))
        y_scr[:, c0:c0 + N_CHUNK] = (mixed * pscale_ref[:, c0:c0 + N_CHUNK] * gate).astype(BF16)

    def q_proj(c):
        q = proj(COL_Q + c * N_CHUNK, N_CHUNK).astype(BF16)
        for g in range(GQA_GROUP):
            q_scr[c * GQA_GROUP + g] = q[:, g * HEAD_DIM:(g + 1) * HEAD_DIM]

    def kv_proj():
        kv = proj(COL_K, 2 * KV_WIDTH)
        for h in range(N_KV_HEADS):
            kext_scr[h, BLOCK:BLOCK + T, :] = kv[:, h * HEAD_DIM:(h + 1) * HEAD_DIM].astype(BF16)
        vext_scr[:, BLOCK:BLOCK + T] = kv[:, KV_WIDTH:2 * KV_WIDTH].T.astype(BF16)

    def scores(h, i):
        first = (s_idx == 0).astype(jnp.int32)
        r0 = i * BLOCK
        qs = q_scr[h * GQA_GROUP:(h + 1) * GQA_GROUP, r0:r0 + BLOCK, :]
        qs = qs.reshape(QCOLS, HEAD_DIM)
        kwin = kext_scr[h, r0:r0 + 2 * BLOCK, :]
        sc = lax.dot_general(kwin, qs, (((1,), (1,)), ((), ())),
                             preferred_element_type=F32)
        return sc + tab_ref[first if i == 0 else 0, h]

    def softmax_values(h, i, sc):
        sink = sink_ref[h]
        r0 = i * BLOCK
        vwin = vext_scr[h * HEAD_DIM:(h + 1) * HEAD_DIM, r0:r0 + 2 * BLOCK]
        m = jnp.maximum(jnp.max(sc, axis=0, keepdims=True), sink)
        e = jnp.exp(sc - m)
        l = jnp.sum(e, axis=0, keepdims=True) + jnp.exp(sink - m)
        o = jnp.dot(vwin, e.astype(BF16), preferred_element_type=F32)
        o = o * (1.0 / l)
        for g in range(GQA_GROUP):
            row = (h * GQA_GROUP + g) * HEAD_DIM
            yattn_scr[row:row + HEAD_DIM, r0:r0 + BLOCK] = o[:, g * BLOCK:(g + 1) * BLOCK]

    def gate_proj(c):
        c0 = c * N_CHUNK
        gate_scr[:, c0:c0 + N_CHUNK] = _silu(proj(COL_GA + c0, N_CHUNK))

    def gate_attention(c):
        c0 = c * N_CHUNK
        y_scr[:, POOL_WIDTH + c0:POOL_WIDTH + c0 + N_CHUNK] = (
            yattn_scr[c0:c0 + N_CHUNK, :].T * gate_scr[:, c0:c0 + N_CHUNK]).astype(BF16)

    def finish():
        gamma = ln_g_ref[...]
        beta = ln_b_ref[...]
        half = T // OUT_SPLIT
        for hf in range(OUT_SPLIT):
            hrows = slice(hf * half, (hf + 1) * half)
            o_ref[hrows, :] = jnp.dot(y_scr[hrows, :], w_out_ref[...], preferred_element_type=F32)
            for r in range(hf * half // LN_ROWS, (hf + 1) * half // LN_ROWS):
                rows = slice(r * LN_ROWS, (r + 1) * LN_ROWS)
                z = DEEPNORM_ALPHA * x_ref[rows, :] + o_ref[rows, :]
                mu = jnp.mean(z, axis=-1, keepdims=True)
                zc = z - mu
                var = jnp.mean(zc * zc, axis=-1, keepdims=True)
                o_ref[rows, :] = zc * lax.rsqrt(var + LN_EPS) * gamma + beta

    fillers = [lambda: pool_u(0), lambda: pool_u(1), lambda: pool_mix(0), lambda: gate_proj(0),
               lambda: pool_mix(1), lambda: gate_proj(1), lambda: gate_attention(0), lambda: None]
    blocks = [(h, i) for h in range(N_KV_HEADS) for i in range(T // BLOCK)]
    kv_proj()
    q_proj(0)
    q_proj(1)
    pending = [scores(*blocks[k]) for k in range(LOOKAHEAD)]
    for k, (h, i) in enumerate(blocks):
        softmax_values(h, i, pending.pop(0))
        if k + LOOKAHEAD < len(blocks):
            pending.append(scores(*blocks[k + LOOKAHEAD]))
        if k % 2 == 1:
            fillers[k // 2]()
    gate_attention(1)
    finish()


def _t5_bucket(dist):
    max_exact = N_BUCKETS // 2
    ratio = (np.log(np.maximum(dist, 1).astype(np.float32) / np.float32(max_exact))
             / np.float32(math.log(MAX_DISTANCE / max_exact)))
    large = max_exact + (ratio * np.float32(N_BUCKETS - max_exact)).astype(np.int32)
    large = np.minimum(large, N_BUCKETS - 1)
    return np.where(dist < max_exact, dist, large)


def _bias_tables(rel_bias):
    kj = np.arange(2 * BLOCK)[:, None]
    qi = np.arange(BLOCK)[None, :]
    dist = qi + BLOCK - kj
    valid = (dist >= 0) & (dist < WINDOW)
    bucket = _t5_bucket(np.clip(dist, 0, None))
    onehot = bucket[None, :, :, None] == np.arange(N_BUCKETS)
    table = rel_bias.astype(F32).T[:, None, None, :]
    bias = jnp.sum(jnp.where(onehot, table, 0.0), axis=-1)
    tab_any = jnp.where(valid[None], bias, NEG_INF)
    tab_first = jnp.where((valid & (kj >= BLOCK))[None], bias, NEG_INF)
    tab = jnp.stack([tab_any, tab_first])
    tab = tab.reshape(2, N_KV_HEADS, GQA_GROUP, 2 * BLOCK, BLOCK)
    return jnp.transpose(tab, (0, 1, 3, 2, 4)).reshape(2, N_KV_HEADS, 2 * BLOCK, QCOLS)


def _query_scale():
    scale = np.ones((1, IN_WIDTH), np.float32)
    scale[:, COL_Q:COL_K] = HEAD_DIM ** -0.5
    return scale


def _pair_block_diag(pool_w):
    z = jnp.zeros((POOL_GROUP, POOL_GROUP), pool_w.dtype)
    pairs = []
    for p in range(2):
        a, b = pool_w[2 * p], pool_w[2 * p + 1]
        pairs.append(jnp.concatenate([jnp.concatenate([a, z], axis=1),
                                      jnp.concatenate([z, b], axis=1)], axis=0))
    return jnp.stack(pairs)


def _layer(x, ln_g, ln_b, w_in, pool_w, pool_scale, sinks, w_out, tab):
    B, S, D = x.shape
    T = TILE
    const = lambda *shape: pl.BlockSpec(shape, lambda b, s: (0,) * len(shape))
    sink_rows = jnp.repeat(sinks.astype(F32), BLOCK).reshape(N_KV_HEADS, 1, QCOLS)
    return pl.pallas_call(
        _layer_kernel,
        out_shape=jax.ShapeDtypeStruct((B, S, D), F32),
        grid=(B, S // T),
        in_specs=[
            pl.BlockSpec((None, T, D), lambda b, s: (b, s, 0)),
            const(D, IN_WIDTH),
            const(2, N_CHUNK, N_CHUNK),
            const(1, POOL_WIDTH),
            const(D, D),
            const(2, N_KV_HEADS, 2 * BLOCK, QCOLS),
            const(N_KV_HEADS, 1, QCOLS),
            const(1, D),
            const(1, D),
        ],
        out_specs=pl.BlockSpec((None, T, D), lambda b, s: (b, s, 0)),
        scratch_shapes=[
            pltpu.VMEM((T, D), BF16),
            pltpu.VMEM((MAX_POOL_WINDOW + T, POOL_WIDTH), F32),
            pltpu.VMEM((N_KV_HEADS, BLOCK + T, HEAD_DIM), BF16),
            pltpu.VMEM((KV_WIDTH, BLOCK + T), BF16),
            pltpu.VMEM((N_Q_HEADS, T, HEAD_DIM), BF16),
            pltpu.VMEM((ATTN_WIDTH, T), F32),
            pltpu.VMEM((T, ATTN_WIDTH), F32),
            pltpu.VMEM((T, D), BF16),
        ],
        compiler_params=pltpu.CompilerParams(
            dimension_semantics=("arbitrary", "arbitrary"),
            vmem_limit_bytes=VMEM_LIMIT_BYTES),
        name="hybrid_layer",
    )(x, (w_in * _query_scale()).astype(BF16), _pair_block_diag(pool_w).astype(BF16),
      pool_scale.reshape(1, POOL_WIDTH), w_out.astype(BF16), tab, sink_rows,
      ln_g.reshape(1, D), ln_b.reshape(1, D))


@jax.jit
def kernel(x, ln_g, ln_b, w_in, pool_w, pool_scale, sinks, w_out, rel_bias):
    tab = _bias_tables(rel_bias)
    for layer in range(DEPTH):
        x = _layer(x, ln_g[layer], ln_b[layer], w_in[layer], pool_w[layer],
                   pool_scale[layer], sinks[layer], w_out[layer], tab)
    return x
```

```python
import math

import jax
import jax.numpy as jnp
from jax import lax
from jax.experimental import pallas as pl
from jax.experimental.pallas import tpu as pltpu
import numpy as np

D_MODEL = 1024
DEPTH = 2
POOL_WIDTH = 512
POOL_WINDOWS = (2, 4, 8, 16)
POOL_GROUP = 128
MAX_POOL_WINDOW = 16
HEAD_DIM = 64
N_Q_HEADS = 8
N_KV_HEADS = 2
GQA_GROUP = 4
ATTN_WIDTH = 512
KV_WIDTH = 128
WINDOW = 128
BLOCK = 128
N_BUCKETS = 32
MAX_DISTANCE = 128
IN_WIDTH = 2304
COL_U, COL_GP, COL_Q, COL_K, COL_V, COL_GA = 0, 512, 1024, 1536, 1664, 1792
DEEPNORM_ALPHA = (2.0 * DEPTH) ** 0.25
LN_EPS = 1e-5
NEG_INF = -1e30
LOG2E = math.log2(math.e)

TILE = 1024
N_CHUNK = 256
LN_ROWS = 64
LOOKAHEAD = 2
OUT_SPLIT = 4
QCOLS = GQA_GROUP * BLOCK
VMEM_LIMIT_BYTES = 56 * 1024 * 1024

F32 = jnp.float32
BF16 = jnp.bfloat16


def _silu(v):
    return v / (1.0 + jnp.exp2(v * (-LOG2E)))


def _layer_kernel(x_ref, w_in_ref, w_u_ref, pw_ref, pscale_ref, w_out_ref, tab_ref, sink_ref,
                  ln_g_ref, ln_b_ref, o_ref,
                  wu_scr, xb_scr, uext_scr, kext_scr, vext_scr, q_scr, yattn_scr, gate_scr, y_scr):
    s_idx = pl.program_id(1)
    T = TILE
    H = MAX_POOL_WINDOW

    @pl.when((pl.program_id(0) == 0) & (s_idx == 0))
    def _():
        for gi in range(len(POOL_WINDOWS)):
            g0 = gi * POOL_GROUP
            wu_scr[:, g0:g0 + POOL_GROUP] = jnp.dot(
                w_u_ref[:, g0:g0 + POOL_GROUP], pw_ref[gi], precision=lax.Precision.HIGHEST,
                preferred_element_type=F32).astype(BF16)

    uext_scr[0:H, :] = uext_scr[T:T + H, :]
    kext_scr[:, 0:BLOCK, :] = kext_scr[:, T:T + BLOCK, :]
    vext_scr[:, 0:BLOCK] = vext_scr[:, T:T + BLOCK]

    @pl.when(s_idx == 0)
    def _():
        uext_scr[0:H, :] = jnp.zeros((H, POOL_WIDTH), F32)
        kext_scr[:, 0:BLOCK, :] = jnp.zeros((N_KV_HEADS, BLOCK, HEAD_DIM), BF16)
        vext_scr[:, 0:BLOCK] = jnp.zeros((KV_WIDTH, BLOCK), BF16)

    xb_scr[...] = x_ref[...].astype(BF16)

    def proj(col, width):
        return jnp.dot(xb_scr[...], w_in_ref[:, col:col + width], preferred_element_type=F32)

    def pool_u(p):
        c0 = p * N_CHUNK
        uext_scr[H:H + T, c0:c0 + N_CHUNK] = jnp.dot(
            xb_scr[...], wu_scr[:, c0:c0 + N_CHUNK], preferred_element_type=F32)

    def pool_mix(p):
        t_top = s_idx * T + lax.broadcasted_iota(jnp.int32, (H, POOL_GROUP), 0)
        c0 = p * N_CHUNK
        mixed = []
        for gi in range(2 * p, 2 * p + 2):
            w = POOL_WINDOWS[gi]
            g0 = gi * POOL_GROUP
            ext = uext_scr[:, g0:g0 + POOL_GROUP]
            acc, span = ext, 1
            while span < w:
                acc = acc + pltpu.roll(acc, span, axis=0)
                span *= 2
            top = acc[H:2 * H] / jnp.minimum(t_top + 1, w).astype(F32)
            mean = jnp.concatenate([top, acc[2 * H:H + T] * (1.0 / w)], axis=0)
            mixed.append(mean - ext[H:H + T])
        mixed = jnp.concatenate(mixed, axis=1)
        gate = _silu(proj(COL_GP + c0, N_CHUNK))
        y_scr[:, c0:c0 + N_CHUNK] = (mixed * pscale_ref[:, c0:c0 + N_CHUNK] * gate).astype(BF16)

    def q_proj(c):
        q = proj(COL_Q + c * N_CHUNK, N_CHUNK).astype(BF16)
        for g in range(GQA_GROUP):
            q_scr[c * GQA_GROUP + g] = q[:, g * HEAD_DIM:(g + 1) * HEAD_DIM]

    def kv_proj():
        kv = proj(COL_K, 2 * KV_WIDTH)
        for h in range(N_KV_HEADS):
            kext_scr[h, BLOCK:BLOCK + T, :] = kv[:, h * HEAD_DIM:(h + 1) * HEAD_DIM].astype(BF16)
        vext_scr[:, BLOCK:BLOCK + T] = kv[:, KV_WIDTH:2 * KV_WIDTH].T.astype(BF16)

    def scores(h, i):
        first = (s_idx == 0).astype(jnp.int32)
        r0 = i * BLOCK
        qs = q_scr[h * GQA_GROUP:(h + 1) * GQA_GROUP, r0:r0 + BLOCK, :]
        qs = qs.reshape(QCOLS, HEAD_DIM)
        kwin = kext_scr[h, r0:r0 + 2 * BLOCK, :]
        sc = lax.dot_general(kwin, qs, (((1,), (1,)), ((), ())),
                             preferred_element_type=F32)
        return sc + tab_ref[first if i == 0 else 0, h]

    def softmax_values(h, i, sc):
        sink = sink_ref[h]
        r0 = i * BLOCK
        vwin = vext_scr[h * HEAD_DIM:(h + 1) * HEAD_DIM, r0:r0 + 2 * BLOCK]
        m = jnp.maximum(jnp.max(sc, axis=0, keepdims=True), sink)
        e = jnp.exp(sc - m)
        l = jnp.sum(e, axis=0, keepdims=True) + jnp.exp(sink - m)
        o = jnp.dot(vwin, e.astype(BF16), preferred_element_type=F32)
        o = o * (1.0 / l)
        for g in range(GQA_GROUP):
            row = (h * GQA_GROUP + g) * HEAD_DIM
            yattn_scr[row:row + HEAD_DIM, r0:r0 + BLOCK] = o[:, g * BLOCK:(g + 1) * BLOCK]

    def gate_proj(c):
        c0 = c * N_CHUNK
        gate_scr[:, c0:c0 + N_CHUNK] = _silu(proj(COL_GA + c0, N_CHUNK))

    def gate_attention(c):
        c0 = c * N_CHUNK
        y_scr[:, POOL_WIDTH + c0:POOL_WIDTH + c0 + N_CHUNK] = (
            yattn_scr[c0:c0 + N_CHUNK, :].T * gate_scr[:, c0:c0 + N_CHUNK]).astype(BF16)

    def finish():
        gamma = ln_g_ref[...]
        beta = ln_b_ref[...]
        half = T // OUT_SPLIT
        for hf in range(OUT_SPLIT):
            hrows = slice(hf * half, (hf + 1) * half)
            o_ref[hrows, :] = jnp.dot(y_scr[hrows, :], w_out_ref[...], preferred_element_type=F32)
            for r in range(hf * half // LN_ROWS, (hf + 1) * half // LN_ROWS):
                rows = slice(r * LN_ROWS, (r + 1) * LN_ROWS)
                z = DEEPNORM_ALPHA * x_ref[rows, :] + o_ref[rows, :]
                mu = jnp.mean(z, axis=-1, keepdims=True)
                zc = z - mu
                var = jnp.mean(zc * zc, axis=-1, keepdims=True)
                o_ref[rows, :] = zc * lax.rsqrt(var + LN_EPS) * gamma + beta

    fillers = [lambda: pool_u(0), lambda: pool_u(1), lambda: pool_mix(0), lambda: gate_proj(0),
               lambda: pool_mix(1), lambda: gate_proj(1), lambda: gate_attention(0), lambda: None]
    blocks = [(h, i) for h in range(N_KV_HEADS) for i in range(T // BLOCK)]
    kv_proj()
    q_proj(0)
    q_proj(1)
    pending = [scores(*blocks[k]) for k in range(LOOKAHEAD)]
    for k, (h, i) in enumerate(blocks):
        softmax_values(h, i, pending.pop(0))
        if k + LOOKAHEAD < len(blocks):
            pending.append(scores(*blocks[k + LOOKAHEAD]))
        if k % 2 == 1:
            fillers[k // 2]()
    gate_attention(1)
    finish()


def _t5_bucket(dist):
    max_exact = N_BUCKETS // 2
    ratio = (np.log(np.maximum(dist, 1).astype(np.float32) / np.float32(max_exact))
             / np.float32(math.log(MAX_DISTANCE / max_exact)))
    large = max_exact + (ratio * np.float32(N_BUCKETS - max_exact)).astype(np.int32)
    large = np.minimum(large, N_BUCKETS - 1)
    return np.where(dist < max_exact, dist, large)


def _bias_tables(rel_bias):
    kj = np.arange(2 * BLOCK)[:, None]
    qi = np.arange(BLOCK)[None, :]
    dist = qi + BLOCK - kj
    valid = (dist >= 0) & (dist < WINDOW)
    bucket = _t5_bucket(np.clip(dist, 0, None))
    onehot = bucket[None, :, :, None] == np.arange(N_BUCKETS)
    table = rel_bias.astype(F32).T[:, None, None, :]
    bias = jnp.sum(jnp.where(onehot, table, 0.0), axis=-1)
    tab_any = jnp.where(valid[None], bias, NEG_INF)
    tab_first = jnp.where((valid & (kj >= BLOCK))[None], bias, NEG_INF)
    tab = jnp.stack([tab_any, tab_first])
    tab = tab.reshape(2, N_KV_HEADS, GQA_GROUP, 2 * BLOCK, BLOCK)
    return jnp.transpose(tab, (0, 1, 3, 2, 4)).reshape(2, N_KV_HEADS, 2 * BLOCK, QCOLS)


def _query_scale():
    scale = np.ones((1, IN_WIDTH), np.float32)
    scale[:, COL_Q:COL_K] = HEAD_DIM ** -0.5
    return scale


def _layer(layer, x, ln_g, ln_b, w_in_all, pool_w, pool_scale, sinks, w_out, tab):
    B, S, D = x.shape
    T = TILE
    w_in = w_in_all[layer]
    const = lambda *shape: pl.BlockSpec(shape, lambda b, s: (0,) * len(shape))
    sink_rows = jnp.repeat(sinks.astype(F32), BLOCK).reshape(N_KV_HEADS, 1, QCOLS)
    return pl.pallas_call(
        _layer_kernel,
        out_shape=jax.ShapeDtypeStruct((B, S, D), F32),
        grid=(B, S // T),
        in_specs=[
            pl.BlockSpec((None, T, D), lambda b, s: (b, s, 0)),
            const(D, IN_WIDTH),
            pl.BlockSpec((None, D, POOL_WIDTH), lambda b, s: (layer, 0, 0)),
            const(len(POOL_WINDOWS), POOL_GROUP, POOL_GROUP),
            const(1, POOL_WIDTH),
            const(D, D),
            const(2, N_KV_HEADS, 2 * BLOCK, QCOLS),
            const(N_KV_HEADS, 1, QCOLS),
            const(1, D),
            const(1, D),
        ],
        out_specs=pl.BlockSpec((None, T, D), lambda b, s: (b, s, 0)),
        scratch_shapes=[
            pltpu.VMEM((D, POOL_WIDTH), BF16),
            pltpu.VMEM((T, D), BF16),
            pltpu.VMEM((MAX_POOL_WINDOW + T, POOL_WIDTH), F32),
            pltpu.VMEM((N_KV_HEADS, BLOCK + T, HEAD_DIM), BF16),
            pltpu.VMEM((KV_WIDTH, BLOCK + T), BF16),
            pltpu.VMEM((N_Q_HEADS, T, HEAD_DIM), BF16),
            pltpu.VMEM((ATTN_WIDTH, T), F32),
            pltpu.VMEM((T, ATTN_WIDTH), F32),
            pltpu.VMEM((T, D), BF16),
        ],
        compiler_params=pltpu.CompilerParams(
            dimension_semantics=("arbitrary", "arbitrary"),
            vmem_limit_bytes=VMEM_LIMIT_BYTES),
        name="hybrid_layer",
    )(x, (w_in * _query_scale()).astype(BF16), w_in_all, pool_w,
      pool_scale.reshape(1, POOL_WIDTH), w_out.astype(BF16), tab, sink_rows,
      ln_g.reshape(1, D), ln_b.reshape(1, D))


@jax.jit
def kernel(x, ln_g, ln_b, w_in, pool_w, pool_scale, sinks, w_out, rel_bias):
    tab = _bias_tables(rel_bias)
    for layer in range(DEPTH):
        x = _layer(layer, x, ln_g[layer], ln_b[layer], w_in, pool_w[layer],
                   pool_scale[layer], sinks[layer], w_out[layer], tab)
    return x
```

```python
import math

import jax
import jax.numpy as jnp
from jax import lax
from jax.experimental import pallas as pl
from jax.experimental.pallas import tpu as pltpu
import numpy as np

D_MODEL = 1024
DEPTH = 2
POOL_WIDTH = 512
POOL_WINDOWS = (2, 4, 8, 16)
POOL_GROUP = 128
MAX_POOL_WINDOW = 16
HEAD_DIM = 64
N_Q_HEADS = 8
N_KV_HEADS = 2
GQA_GROUP = 4
ATTN_WIDTH = 512
KV_WIDTH = 128
WINDOW = 128
BLOCK = 128
N_BUCKETS = 32
MAX_DISTANCE = 128
IN_WIDTH = 2304
COL_U, COL_GP, COL_Q, COL_K, COL_V, COL_GA = 0, 512, 1024, 1536, 1664, 1792
DEEPNORM_ALPHA = (2.0 * DEPTH) ** 0.25
LN_EPS = 1e-5
NEG_INF = -1e30
LOG2E = math.log2(math.e)

TILE = 1024
N_CHUNK = 256
LN_ROWS = 64
LOOKAHEAD = 3
OUT_SPLIT = 4
QCOLS = GQA_GROUP * BLOCK
VMEM_LIMIT_BYTES = 56 * 1024 * 1024

F32 = jnp.float32
BF16 = jnp.bfloat16


def _silu(v):
    return v / (1.0 + jnp.exp2(v * (-LOG2E)))


def _layer_kernel(x_ref, w_in_ref, pw_ref, pscale_ref, w_out_ref, tab_ref, sink_ref,
                  ln_g_ref, ln_b_ref, o_ref,
                  xb_scr, uext_scr, kext_scr, vext_scr, q_scr, yattn_scr, gate_scr, y_scr):
    s_idx = pl.program_id(1)
    T = TILE
    H = MAX_POOL_WINDOW

    uext_scr[0:H, :] = uext_scr[T:T + H, :]
    kext_scr[:, 0:BLOCK, :] = kext_scr[:, T:T + BLOCK, :]
    vext_scr[:, 0:BLOCK] = vext_scr[:, T:T + BLOCK]

    @pl.when(s_idx == 0)
    def _():
        uext_scr[0:H, :] = jnp.zeros((H, POOL_WIDTH), F32)
        kext_scr[:, 0:BLOCK, :] = jnp.zeros((N_KV_HEADS, BLOCK, HEAD_DIM), BF16)
        vext_scr[:, 0:BLOCK] = jnp.zeros((KV_WIDTH, BLOCK), BF16)

    xb_scr[...] = x_ref[...].astype(BF16)

    def proj(col, width):
        return jnp.dot(xb_scr[...], w_in_ref[:, col:col + width], preferred_element_type=F32)

    def pool_u(p):
        c0 = p * N_CHUNK
        uext_scr[H:H + T, c0:c0 + N_CHUNK] = proj(COL_U + c0, N_CHUNK)

    def pool_mix(p):
        t_top = s_idx * T + lax.broadcasted_iota(jnp.int32, (H, POOL_GROUP), 0)
        c0 = p * N_CHUNK
        pooled = []
        for gi in range(2 * p, 2 * p + 2):
            w = POOL_WINDOWS[gi]
            g0 = gi * POOL_GROUP
            ext = uext_scr[:, g0:g0 + POOL_GROUP]
            acc, span = ext, 1
            while span < w:
                acc = acc + pltpu.roll(acc, span, axis=0)
                span *= 2
            top = acc[H:2 * H] / jnp.minimum(t_top + 1, w).astype(F32)
            mean = jnp.concatenate([top, acc[2 * H:H + T] * (1.0 / w)], axis=0)
            pooled.append((mean - ext[H:H + T]).astype(BF16))
        pooled = jnp.concatenate(pooled, axis=1)
        mixed = jnp.dot(pooled, pw_ref[p], preferred_element_type=F32)
        gate = _silu(proj(COL_GP + c0, N_CHUNK))
        y_scr[:, c0:c0 + N_CHUNK] = (mixed * pscale_ref[:, c0:c0 + N_CHUNK] * gate).astype(BF16)

    def q_proj(c):
        q = proj(COL_Q + c * N_CHUNK, N_CHUNK).astype(BF16)
        for g in range(GQA_GROUP):
            q_scr[c * GQA_GROUP + g] = q[:, g * HEAD_DIM:(g + 1) * HEAD_DIM]

    def kv_proj():
        kv = proj(COL_K, 2 * KV_WIDTH)
        for h in range(N_KV_HEADS):
            kext_scr[h, BLOCK:BLOCK + T, :] = kv[:, h * HEAD_DIM:(h + 1) * HEAD_DIM].astype(BF16)
        vext_scr[:, BLOCK:BLOCK + T] = kv[:, KV_WIDTH:2 * KV_WIDTH].T.astype(BF16)

    def scores(h, i):
        first = (s_idx == 0).astype(jnp.int32)
        r0 = i * BLOCK
        qs = q_scr[h * GQA_GROUP:(h + 1) * GQA_GROUP, r0:r0 + BLOCK, :]
        qs = qs.reshape(QCOLS, HEAD_DIM)
        kwin = kext_scr[h, r0:r0 + 2 * BLOCK, :]
        sc = lax.dot_general(kwin, qs, (((1,), (1,)), ((), ())),
                             preferred_element_type=F32)
        return sc + tab_ref[first if i == 0 else 0, h]

    def softmax_values(h, i, sc):
        sink = sink_ref[h]
        r0 = i * BLOCK
        vwin = vext_scr[h * HEAD_DIM:(h + 1) * HEAD_DIM, r0:r0 + 2 * BLOCK]
        m = jnp.maximum(jnp.max(sc, axis=0, keepdims=True), sink)
        e = jnp.exp(sc - m)
        l = jnp.sum(e, axis=0, keepdims=True) + jnp.exp(sink - m)
        o = jnp.dot(vwin, e.astype(BF16), preferred_element_type=F32)
        o = o * (1.0 / l)
        for g in range(GQA_GROUP):
            row = (h * GQA_GROUP + g) * HEAD_DIM
            yattn_scr[row:row + HEAD_DIM, r0:r0 + BLOCK] = o[:, g * BLOCK:(g + 1) * BLOCK]

    def gate_proj(c):
        c0 = c * N_CHUNK
        gate_scr[:, c0:c0 + N_CHUNK] = _silu(proj(COL_GA + c0, N_CHUNK))

    def gate_attention(c):
        c0 = c * N_CHUNK
        y_scr[:, POOL_WIDTH + c0:POOL_WIDTH + c0 + N_CHUNK] = (
            yattn_scr[c0:c0 + N_CHUNK, :].T * gate_scr[:, c0:c0 + N_CHUNK]).astype(BF16)

    def finish():
        gamma = ln_g_ref[...]
        beta = ln_b_ref[...]
        half = T // OUT_SPLIT
        for hf in range(OUT_SPLIT):
            hrows = slice(hf * half, (hf + 1) * half)
            o_ref[hrows, :] = jnp.dot(y_scr[hrows, :], w_out_ref[...], preferred_element_type=F32)
            for r in range(hf * half // LN_ROWS, (hf + 1) * half // LN_ROWS):
                rows = slice(r * LN_ROWS, (r + 1) * LN_ROWS)
                z = DEEPNORM_ALPHA * x_ref[rows, :] + o_ref[rows, :]
                mu = jnp.mean(z, axis=-1, keepdims=True)
                zc = z - mu
                var = jnp.mean(zc * zc, axis=-1, keepdims=True)
                o_ref[rows, :] = zc * lax.rsqrt(var + LN_EPS) * gamma + beta

    fillers = [lambda: pool_u(0), lambda: pool_u(1), lambda: pool_mix(0), lambda: gate_proj(0),
               lambda: pool_mix(1), lambda: gate_proj(1), lambda: gate_attention(0), lambda: None]
    blocks = [(h, i) for h in range(N_KV_HEADS) for i in range(T // BLOCK)]
    kv_proj()
    q_proj(0)
    q_proj(1)
    pending = [scores(*blocks[k]) for k in range(LOOKAHEAD)]
    for k, (h, i) in enumerate(blocks):
        softmax_values(h, i, pending.pop(0))
        if k + LOOKAHEAD < len(blocks):
            pending.append(scores(*blocks[k + LOOKAHEAD]))
        if k % 2 == 1:
            fillers[k // 2]()
    gate_attention(1)
    finish()


def _t5_bucket(dist):
    max_exact = N_BUCKETS // 2
    ratio = (np.log(np.maximum(dist, 1).astype(np.float32) / np.float32(max_exact))
             / np.float32(math.log(MAX_DISTANCE / max_exact)))
    large = max_exact + (ratio * np.float32(N_BUCKETS - max_exact)).astype(np.int32)
    large = np.minimum(large, N_BUCKETS - 1)
    return np.where(dist < max_exact, dist, large)


def _bias_tables(rel_bias):
    kj = np.arange(2 * BLOCK)[:, None]
    qi = np.arange(BLOCK)[None, :]
    dist = qi + BLOCK - kj
    valid = (dist >= 0) & (dist < WINDOW)
    bucket = _t5_bucket(np.clip(dist, 0, None))
    onehot = bucket[None, :, :, None] == np.arange(N_BUCKETS)
    table = rel_bias.astype(F32).T[:, None, None, :]
    bias = jnp.sum(jnp.where(onehot, table, 0.0), axis=-1)
    tab_any = jnp.where(valid[None], bias, NEG_INF)
    tab_first = jnp.where((valid & (kj >= BLOCK))[None], bias, NEG_INF)
    tab = jnp.stack([tab_any, tab_first])
    tab = tab.reshape(2, N_KV_HEADS, GQA_GROUP, 2 * BLOCK, BLOCK)
    return jnp.transpose(tab, (0, 1, 3, 2, 4)).reshape(2, N_KV_HEADS, 2 * BLOCK, QCOLS)


def _query_scale():
    scale = np.ones((1, IN_WIDTH), np.float32)
    scale[:, COL_Q:COL_K] = HEAD_DIM ** -0.5
    return scale


def _pair_block_diag(pool_w):
    z = jnp.zeros((POOL_GROUP, POOL_GROUP), pool_w.dtype)
    pairs = []
    for p in range(2):
        a, b = pool_w[2 * p], pool_w[2 * p + 1]
        pairs.append(jnp.concatenate([jnp.concatenate([a, z], axis=1),
                                      jnp.concatenate([z, b], axis=1)], axis=0))
    return jnp.stack(pairs)


def _layer(x, ln_g, ln_b, w_in, pool_w, pool_scale, sinks, w_out, tab):
    B, S, D = x.shape
    T = TILE
    const = lambda *shape: pl.BlockSpec(shape, lambda b, s: (0,) * len(shape))
    sink_rows = jnp.repeat(sinks.astype(F32), BLOCK).reshape(N_KV_HEADS, 1, QCOLS)
    return pl.pallas_call(
        _layer_kernel,
        out_shape=jax.ShapeDtypeStruct((B, S, D), F32),
        grid=(B, S // T),
        in_specs=[
            pl.BlockSpec((None, T, D), lambda b, s: (b, s, 0)),
            const(D, IN_WIDTH),
            const(2, N_CHUNK, N_CHUNK),
            const(1, POOL_WIDTH),
            const(D, D),
            const(2, N_KV_HEADS, 2 * BLOCK, QCOLS),
            const(N_KV_HEADS, 1, QCOLS),
            const(1, D),
            const(1, D),
        ],
        out_specs=pl.BlockSpec((None, T, D), lambda b, s: (b, s, 0)),
        scratch_shapes=[
            pltpu.VMEM((T, D), BF16),
            pltpu.VMEM((MAX_POOL_WINDOW + T, POOL_WIDTH), F32),
            pltpu.VMEM((N_KV_HEADS, BLOCK + T, HEAD_DIM), BF16),
            pltpu.VMEM((KV_WIDTH, BLOCK + T), BF16),
            pltpu.VMEM((N_Q_HEADS, T, HEAD_DIM), BF16),
            pltpu.VMEM((ATTN_WIDTH, T), F32),
            pltpu.VMEM((T, ATTN_WIDTH), F32),
            pltpu.VMEM((T, D), BF16),
        ],
        compiler_params=pltpu.CompilerParams(
            dimension_semantics=("arbitrary", "arbitrary"),
            vmem_limit_bytes=VMEM_LIMIT_BYTES),
        name="hybrid_layer",
    )(x, (w_in * _query_scale()).astype(BF16), _pair_block_diag(pool_w).astype(BF16),
      pool_scale.reshape(1, POOL_WIDTH), w_out.astype(BF16), tab, sink_rows,
      ln_g.reshape(1, D), ln_b.reshape(1, D))


@jax.jit
def kernel(x, ln_g, ln_b, w_in, pool_w, pool_scale, sinks, w_out, rel_bias):
    tab = _bias_tables(rel_bias)
    for layer in range(DEPTH):
        x = _layer(x, ln_g[layer], ln_b[layer], w_in[layer], pool_w[layer],
                   pool_scale[layer], sinks[layer], w_out[layer], tab)
    return x
```

```python
import math

import jax
import jax.numpy as jnp
from jax import lax
from jax.experimental import pallas as pl
from jax.experimental.pallas import tpu as pltpu
import numpy as np

D_MODEL = 1024
DEPTH = 2
POOL_WIDTH = 512
POOL_WINDOWS = (2, 4, 8, 16)
POOL_GROUP = 128
MAX_POOL_WINDOW = 16
HEAD_DIM = 64
N_Q_HEADS = 8
N_KV_HEADS = 2
GQA_GROUP = 4
ATTN_WIDTH = 512
KV_WIDTH = 128
WINDOW = 128
BLOCK = 128
N_BUCKETS = 32
MAX_DISTANCE = 128
IN_WIDTH = 2304
COL_U, COL_GP, COL_Q, COL_K, COL_V, COL_GA = 0, 512, 1024, 1536, 1664, 1792
DEEPNORM_ALPHA = (2.0 * DEPTH) ** 0.25
LN_EPS = 1e-5
NEG_INF = -1e30
LOG2E = math.log2(math.e)

TILE = 1024
N_CHUNK = 256
LN_ROWS = 64
LOOKAHEAD = 2
OUT_SPLIT = 4
QCOLS = GQA_GROUP * BLOCK
VMEM_LIMIT_BYTES = 56 * 1024 * 1024

F32 = jnp.float32
BF16 = jnp.bfloat16


def _silu(v):
    return v / (1.0 + jnp.exp2(v * (-LOG2E)))


def _layer_kernel(x_ref, w_in_ref, pw_ref, pscale_ref, w_out_ref, tab_ref, sink_ref,
                  ln_g_ref, ln_b_ref, o_ref,
                  xb_scr, uext_scr, kext_scr, vext_scr, q_scr, yattn_scr, gate_scr, y_scr):
    s_idx = pl.program_id(1)
    T = TILE
    H = MAX_POOL_WINDOW

    uext_scr[0:H, :] = uext_scr[T:T + H, :]
    kext_scr[:, 0:BLOCK, :] = kext_scr[:, T:T + BLOCK, :]
    vext_scr[:, 0:BLOCK] = vext_scr[:, T:T + BLOCK]

    @pl.when(s_idx == 0)
    def _():
        uext_scr[0:H, :] = jnp.zeros((H, POOL_WIDTH), F32)
        kext_scr[:, 0:BLOCK, :] = jnp.zeros((N_KV_HEADS, BLOCK, HEAD_DIM), BF16)
        vext_scr[:, 0:BLOCK] = jnp.zeros((KV_WIDTH, BLOCK), BF16)

    xb_scr[...] = x_ref[...].astype(BF16)

    def proj(col, width):
        return jnp.dot(xb_scr[...], w_in_ref[:, col:col + width], preferred_element_type=F32)

    def pool_u(p):
        c0 = p * N_CHUNK
        uext_scr[H:H + T, c0:c0 + N_CHUNK] = proj(COL_U + c0, N_CHUNK)

    def pool_mix(p):
        t_top = s_idx * T + lax.broadcasted_iota(jnp.int32, (H, POOL_GROUP), 0)
        c0 = p * N_CHUNK
        pooled = []
        for gi in range(2 * p, 2 * p + 2):
            w = POOL_WINDOWS[gi]
            g0 = gi * POOL_GROUP
            ext = uext_scr[:, g0:g0 + POOL_GROUP]
            acc, span = ext, 1
            while span < w:
                acc = acc + pltpu.roll(acc, span, axis=0)
                span *= 2
            top = acc[H:2 * H] / jnp.minimum(t_top + 1, w).astype(F32)
            mean = jnp.concatenate([top, acc[2 * H:H + T] * (1.0 / w)], axis=0)
            pooled.append((mean - ext[H:H + T]).astype(BF16))
        pooled = jnp.concatenate(pooled, axis=1)
        pw_a, pw_b = pw_ref[2 * p], pw_ref[2 * p + 1]
        zeros = jnp.zeros_like(pw_a)
        pw = jnp.concatenate([jnp.concatenate([pw_a, zeros], axis=1),
                              jnp.concatenate([zeros, pw_b], axis=1)], axis=0)
        mixed = jnp.dot(pooled, pw, preferred_element_type=F32)
        gate = _silu(proj(COL_GP + c0, N_CHUNK))
        y_scr[:, c0:c0 + N_CHUNK] = (mixed * pscale_ref[:, c0:c0 + N_CHUNK] * gate).astype(BF16)

    def q_proj(c):
        q = proj(COL_Q + c * N_CHUNK, N_CHUNK).astype(BF16)
        for g in range(GQA_GROUP):
            q_scr[c * GQA_GROUP + g] = q[:, g * HEAD_DIM:(g + 1) * HEAD_DIM]

    def kv_proj():
        kv = proj(COL_K, 2 * KV_WIDTH)
        for h in range(N_KV_HEADS):
            kext_scr[h, BLOCK:BLOCK + T, :] = kv[:, h * HEAD_DIM:(h + 1) * HEAD_DIM].astype(BF16)
        vext_scr[:, BLOCK:BLOCK + T] = kv[:, KV_WIDTH:2 * KV_WIDTH].T.astype(BF16)

    def scores(h, i):
        first = (s_idx == 0).astype(jnp.int32)
        r0 = i * BLOCK
        qs = q_scr[h * GQA_GROUP:(h + 1) * GQA_GROUP, r0:r0 + BLOCK, :]
        qs = qs.reshape(QCOLS, HEAD_DIM)
        kwin = kext_scr[h, r0:r0 + 2 * BLOCK, :]
        sc = lax.dot_general(kwin, qs, (((1,), (1,)), ((), ())),
                             preferred_element_type=F32)
        return sc + tab_ref[first if i == 0 else 0, h]

    def softmax_values(h, i, sc):
        sink = sink_ref[h]
        r0 = i * BLOCK
        vwin = vext_scr[h * HEAD_DIM:(h + 1) * HEAD_DIM, r0:r0 + 2 * BLOCK]
        m = jnp.maximum(jnp.max(sc, axis=0, keepdims=True), sink)
        e = jnp.exp(sc - m)
        l = jnp.sum(e, axis=0, keepdims=True) + jnp.exp(sink - m)
        o = jnp.dot(vwin, e.astype(BF16), preferred_element_type=F32)
        o = o * (1.0 / l)
        for g in range(GQA_GROUP):
            row = (h * GQA_GROUP + g) * HEAD_DIM
            yattn_scr[row:row + HEAD_DIM, r0:r0 + BLOCK] = o[:, g * BLOCK:(g + 1) * BLOCK]

    def gate_proj(c):
        c0 = c * N_CHUNK
        gate_scr[:, c0:c0 + N_CHUNK] = _silu(proj(COL_GA + c0, N_CHUNK))

    def gate_attention(c):
        c0 = c * N_CHUNK
        y_scr[:, POOL_WIDTH + c0:POOL_WIDTH + c0 + N_CHUNK] = (
            yattn_scr[c0:c0 + N_CHUNK, :].T * gate_scr[:, c0:c0 + N_CHUNK]).astype(BF16)

    def finish():
        gamma = ln_g_ref[...]
        beta = ln_b_ref[...]
        half = T // OUT_SPLIT
        for hf in range(OUT_SPLIT):
            hrows = slice(hf * half, (hf + 1) * half)
            o_ref[hrows, :] = jnp.dot(y_scr[hrows, :], w_out_ref[...], preferred_element_type=F32)
            for r in range(hf * half // LN_ROWS, (hf + 1) * half // LN_ROWS):
                rows = slice(r * LN_ROWS, (r + 1) * LN_ROWS)
                z = DEEPNORM_ALPHA * x_ref[rows, :] + o_ref[rows, :]
                mu = jnp.mean(z, axis=-1, keepdims=True)
                zc = z - mu
                var = jnp.mean(zc * zc, axis=-1, keepdims=True)
                o_ref[rows, :] = zc * lax.rsqrt(var + LN_EPS) * gamma + beta

    fillers = [lambda: pool_u(0), lambda: pool_u(1), lambda: pool_mix(0), lambda: gate_proj(0),
               lambda: pool_mix(1), lambda: gate_proj(1), lambda: gate_attention(0), lambda: None]
    blocks = [(h, i) for h in range(N_KV_HEADS) for i in range(T // BLOCK)]
    kv_proj()
    q_proj(0)
    q_proj(1)
    pending = [scores(*blocks[k]) for k in range(LOOKAHEAD)]
    for k, (h, i) in enumerate(blocks):
        softmax_values(h, i, pending.pop(0))
        if k + LOOKAHEAD < len(blocks):
            pending.append(scores(*blocks[k + LOOKAHEAD]))
        if k % 2 == 1:
            fillers[k // 2]()
    gate_attention(1)
    finish()


def _t5_bucket(dist):
    max_exact = N_BUCKETS // 2
    ratio = (np.log(np.maximum(dist, 1).astype(np.float32) / np.float32(max_exact))
             / np.float32(math.log(MAX_DISTANCE / max_exact)))
    large = max_exact + (ratio * np.float32(N_BUCKETS - max_exact)).astype(np.int32)
    large = np.minimum(large, N_BUCKETS - 1)
    return np.where(dist < max_exact, dist, large)


def _bias_tables(rel_bias):
    n_keys = 2 * BLOCK
    onehot = _t5_bucket(np.arange(WINDOW))[:, None] == np.arange(N_BUCKETS)
    by_dist = jnp.sum(jnp.where(onehot[None], rel_bias.astype(F32).T[:, None, :], 0.0),
                      axis=-1)
    period = n_keys + BLOCK + 1
    vec = jnp.concatenate(
        [by_dist, jnp.full((N_Q_HEADS, period - WINDOW), NEG_INF, F32)], axis=1)
    rolled = jnp.tile(vec, (1, n_keys))[:, :n_keys * (period - 1)]
    rolled = rolled.reshape(N_Q_HEADS, n_keys, period - 1)
    tab_any = rolled[:, :, BLOCK:2 * BLOCK]
    tab_first = jnp.where((np.arange(n_keys) >= BLOCK)[None, :, None], tab_any, NEG_INF)
    tab = jnp.stack([tab_any, tab_first])
    tab = tab.reshape(2, N_KV_HEADS, GQA_GROUP, 2 * BLOCK, BLOCK)
    return jnp.transpose(tab, (0, 1, 3, 2, 4)).reshape(2, N_KV_HEADS, 2 * BLOCK, QCOLS)


def _query_scale():
    scale = np.ones((1, IN_WIDTH), np.float32)
    scale[:, COL_Q:COL_K] = HEAD_DIM ** -0.5
    return scale


def _layer(layer, x, params, tab):
    B, S, D = x.shape
    T = TILE
    const = lambda *shape: pl.BlockSpec(shape, lambda b, s: (0,) * len(shape))
    of_layer = lambda *shape: pl.BlockSpec((None,) + shape, lambda b, s: (layer,) + (0,) * len(shape))
    return pl.pallas_call(
        _layer_kernel,
        out_shape=jax.ShapeDtypeStruct((B, S, D), F32),
        grid=(B, S // T),
        in_specs=[
            pl.BlockSpec((None, T, D), lambda b, s: (b, s, 0)),
            of_layer(D, IN_WIDTH),
            of_layer(len(POOL_WINDOWS), POOL_GROUP, POOL_GROUP),
            of_layer(1, POOL_WIDTH),
            of_layer(D, D),
            const(2, N_KV_HEADS, 2 * BLOCK, QCOLS),
            of_layer(N_KV_HEADS, 1, QCOLS),
            of_layer(1, D),
            of_layer(1, D),
        ],
        out_specs=pl.BlockSpec((None, T, D), lambda b, s: (b, s, 0)),
        scratch_shapes=[
            pltpu.VMEM((T, D), BF16),
            pltpu.VMEM((MAX_POOL_WINDOW + T, POOL_WIDTH), F32),
            pltpu.VMEM((N_KV_HEADS, BLOCK + T, HEAD_DIM), BF16),
            pltpu.VMEM((KV_WIDTH, BLOCK + T), BF16),
            pltpu.VMEM((N_Q_HEADS, T, HEAD_DIM), BF16),
            pltpu.VMEM((ATTN_WIDTH, T), F32),
            pltpu.VMEM((T, ATTN_WIDTH), F32),
            pltpu.VMEM((T, D), BF16),
        ],
        compiler_params=pltpu.CompilerParams(
            dimension_semantics=("arbitrary", "arbitrary"),
            vmem_limit_bytes=VMEM_LIMIT_BYTES),
        name="hybrid_layer",
    )(x, *params[:4], tab, *params[4:])


@jax.jit
def kernel(x, ln_g, ln_b, w_in, pool_w, pool_scale, sinks, w_out, rel_bias):
    D = x.shape[-1]
    sink_rows = jnp.repeat(sinks.astype(F32), BLOCK, axis=1).reshape(DEPTH, N_KV_HEADS, 1, QCOLS)
    params = ((w_in * _query_scale()).astype(BF16), pool_w.astype(BF16),
              pool_scale.reshape(DEPTH, 1, POOL_WIDTH), w_out.astype(BF16),
              sink_rows, ln_g.reshape(DEPTH, 1, D), ln_b.reshape(DEPTH, 1, D))
    tab = _bias_tables(rel_bias)
    for layer in range(DEPTH):
        x = _layer(layer, x, params, tab)
    return x
```

```python
import math

import jax
import jax.numpy as jnp
from jax import lax
from jax.experimental import pallas as pl
from jax.experimental.pallas import tpu as pltpu
import numpy as np

D_MODEL = 1024
DEPTH = 2
POOL_WIDTH = 512
POOL_WINDOWS = (2, 4, 8, 16)
POOL_GROUP = 128
MAX_POOL_WINDOW = 16
HEAD_DIM = 64
N_Q_HEADS = 8
N_KV_HEADS = 2
GQA_GROUP = 4
ATTN_WIDTH = 512
KV_WIDTH = 128
WINDOW = 128
BLOCK = 128
N_BUCKETS = 32
MAX_DISTANCE = 128
IN_WIDTH = 2304
COL_U, COL_GP, COL_Q, COL_K, COL_V, COL_GA = 0, 512, 1024, 1536, 1664, 1792
DEEPNORM_ALPHA = (2.0 * DEPTH) ** 0.25
LN_EPS = 1e-5
NEG_INF = -1e30
LOG2E = math.log2(math.e)

TILE = 1024
N_CHUNK = 256
LN_ROWS = 64
LOOKAHEAD = 2
OUT_SPLIT = 4
QCOLS = GQA_GROUP * BLOCK
VMEM_LIMIT_BYTES = 56 * 1024 * 1024

F32 = jnp.float32
BF16 = jnp.bfloat16


def _silu(v):
    return v / (1.0 + jnp.exp2(v * (-LOG2E)))


def _layer_kernel(x_ref, w_in_ref, pw_ref, pscale_ref, w_out_ref, tab_ref, sink_ref,
                  ln_g_ref, ln_b_ref, o_ref,
                  xb_scr, uext_scr, kext_scr, vext_scr, q_scr, yattn_scr, gate_scr, y_scr):
    s_idx = pl.program_id(1)
    T = TILE
    H = MAX_POOL_WINDOW

    uext_scr[0:H, :] = uext_scr[T:T + H, :]
    kext_scr[:, 0:BLOCK, :] = kext_scr[:, T:T + BLOCK, :]
    vext_scr[:, 0:BLOCK] = vext_scr[:, T:T + BLOCK]

    @pl.when(s_idx == 0)
    def _():
        uext_scr[0:H, :] = jnp.zeros((H, POOL_WIDTH), F32)
        kext_scr[:, 0:BLOCK, :] = jnp.zeros((N_KV_HEADS, BLOCK, HEAD_DIM), BF16)
        vext_scr[:, 0:BLOCK] = jnp.zeros((KV_WIDTH, BLOCK), BF16)

    xb_scr[...] = x_ref[...].astype(BF16)

    def proj(col, width):
        return jnp.dot(xb_scr[...], w_in_ref[:, col:col + width], preferred_element_type=F32)

    def pool_u(p):
        c0 = p * N_CHUNK
        uext_scr[H:H + T, c0:c0 + N_CHUNK] = proj(COL_U + c0, N_CHUNK)

    def pool_mix(p):
        t_top = s_idx * T + lax.broadcasted_iota(jnp.int32, (H, POOL_GROUP), 0)
        c0 = p * N_CHUNK
        pooled = []
        for gi in range(2 * p, 2 * p + 2):
            w = POOL_WINDOWS[gi]
            g0 = gi * POOL_GROUP
            ext = uext_scr[:, g0:g0 + POOL_GROUP]
            acc, span = ext, 1
            while span < w:
                acc = acc + pltpu.roll(acc, span, axis=0)
                span *= 2
            top = acc[H:2 * H] / jnp.minimum(t_top + 1, w).astype(F32)
            mean = jnp.concatenate([top, acc[2 * H:H + T] * (1.0 / w)], axis=0)
            pooled.append((mean - ext[H:H + T]).astype(BF16))
        pooled = jnp.concatenate(pooled, axis=1)
        pw_a, pw_b = pw_ref[2 * p], pw_ref[2 * p + 1]
        zeros = jnp.zeros_like(pw_a)
        pw = jnp.concatenate([jnp.concatenate([pw_a, zeros], axis=1),
                              jnp.concatenate([zeros, pw_b], axis=1)], axis=0)
        mixed = jnp.dot(pooled, pw, preferred_element_type=F32)
        gate = _silu(proj(COL_GP + c0, N_CHUNK))
        y_scr[:, c0:c0 + N_CHUNK] = (mixed * pscale_ref[:, c0:c0 + N_CHUNK] * gate).astype(BF16)

    def q_proj(c):
        q = proj(COL_Q + c * N_CHUNK, N_CHUNK).astype(BF16)
        for g in range(GQA_GROUP):
            q_scr[c * GQA_GROUP + g] = q[:, g * HEAD_DIM:(g + 1) * HEAD_DIM]

    def kv_proj():
        kv = proj(COL_K, 2 * KV_WIDTH)
        for h in range(N_KV_HEADS):
            kext_scr[h, BLOCK:BLOCK + T, :] = kv[:, h * HEAD_DIM:(h + 1) * HEAD_DIM].astype(BF16)
        vext_scr[:, BLOCK:BLOCK + T] = kv[:, KV_WIDTH:2 * KV_WIDTH].T.astype(BF16)

    def scores(h, i):
        first = (s_idx == 0).astype(jnp.int32)
        r0 = i * BLOCK
        qs = q_scr[h * GQA_GROUP:(h + 1) * GQA_GROUP, r0:r0 + BLOCK, :]
        qs = qs.reshape(QCOLS, HEAD_DIM)
        kwin = kext_scr[h, r0:r0 + 2 * BLOCK, :]
        sc = lax.dot_general(kwin, qs, (((1,), (1,)), ((), ())),
                             preferred_element_type=F32)
        return sc + tab_ref[first if i == 0 else 0, h]

    def softmax_values(h, i, sc):
        sink = sink_ref[h]
        r0 = i * BLOCK
        vwin = vext_scr[h * HEAD_DIM:(h + 1) * HEAD_DIM, r0:r0 + 2 * BLOCK]
        m = jnp.maximum(jnp.max(sc, axis=0, keepdims=True), sink)
        e = jnp.exp(sc - m)
        l = jnp.sum(e, axis=0, keepdims=True) + jnp.exp(sink - m)
        o = jnp.dot(vwin, e.astype(BF16), preferred_element_type=F32)
        o = o * (1.0 / l)
        for g in range(GQA_GROUP):
            row = (h * GQA_GROUP + g) * HEAD_DIM
            yattn_scr[row:row + HEAD_DIM, r0:r0 + BLOCK] = o[:, g * BLOCK:(g + 1) * BLOCK]

    def gate_proj(c):
        c0 = c * N_CHUNK
        gate_scr[:, c0:c0 + N_CHUNK] = _silu(proj(COL_GA + c0, N_CHUNK))

    def gate_attention(c):
        c0 = c * N_CHUNK
        y_scr[:, POOL_WIDTH + c0:POOL_WIDTH + c0 + N_CHUNK] = (
            yattn_scr[c0:c0 + N_CHUNK, :].T * gate_scr[:, c0:c0 + N_CHUNK]).astype(BF16)

    def finish():
        gamma = ln_g_ref[...]
        beta = ln_b_ref[...]
        half = T // OUT_SPLIT
        for hf in range(OUT_SPLIT):
            hrows = slice(hf * half, (hf + 1) * half)
            o_ref[hrows, :] = jnp.dot(y_scr[hrows, :], w_out_ref[...], preferred_element_type=F32)
            for r in range(hf * half // LN_ROWS, (hf + 1) * half // LN_ROWS):
                rows = slice(r * LN_ROWS, (r + 1) * LN_ROWS)
                z = DEEPNORM_ALPHA * x_ref[rows, :] + o_ref[rows, :]
                mu = jnp.mean(z, axis=-1, keepdims=True)
                zc = z - mu
                var = jnp.mean(zc * zc, axis=-1, keepdims=True)
                o_ref[rows, :] = zc * lax.rsqrt(var + LN_EPS) * gamma + beta

    fillers = [lambda: pool_u(0), lambda: pool_u(1), lambda: pool_mix(0), lambda: gate_proj(0),
               lambda: pool_mix(1), lambda: gate_proj(1), lambda: gate_attention(0), lambda: None]
    blocks = [(h, i) for h in range(N_KV_HEADS) for i in range(T // BLOCK)]
    kv_proj()
    q_proj(0)
    q_proj(1)
    pending = [scores(*blocks[k]) for k in range(LOOKAHEAD)]
    for k, (h, i) in enumerate(blocks):
        softmax_values(h, i, pending.pop(0))
        if k + LOOKAHEAD < len(blocks):
            pending.append(scores(*blocks[k + LOOKAHEAD]))
        if k % 2 == 1:
            fillers[k // 2]()
    gate_attention(1)
    finish()


def _t5_bucket(dist):
    max_exact = N_BUCKETS // 2
    ratio = (np.log(np.maximum(dist, 1).astype(np.float32) / np.float32(max_exact))
             / np.float32(math.log(MAX_DISTANCE / max_exact)))
    large = max_exact + (ratio * np.float32(N_BUCKETS - max_exact)).astype(np.int32)
    large = np.minimum(large, N_BUCKETS - 1)
    return np.where(dist < max_exact, dist, large)


def _table_kernel(by_dist_ref, tab_ref):
    n_keys, lanes = 2 * BLOCK, 4 * BLOCK
    key = lax.broadcasted_iota(jnp.int32, (n_keys, BLOCK), 0)
    for hg in range(N_Q_HEADS):
        h, g = divmod(hg, GQA_GROUP)
        vec = jnp.concatenate([by_dist_ref[hg:hg + 1, :],
                               jnp.full((1, lanes - WINDOW), NEG_INF, F32)], axis=1)
        rolled = pltpu.roll(jnp.broadcast_to(vec, (n_keys, lanes)), 0, 1, stride=1, stride_axis=0)
        visible = rolled[:, BLOCK:2 * BLOCK]
        tab_ref[0, h, :, g * BLOCK:(g + 1) * BLOCK] = visible
        tab_ref[1, h, :, g * BLOCK:(g + 1) * BLOCK] = jnp.where(key >= BLOCK, visible, NEG_INF)


def _bias_tables(rel_bias):
    onehot = _t5_bucket(np.arange(WINDOW))[:, None] == np.arange(N_BUCKETS)
    by_dist = jnp.sum(jnp.where(onehot[None], rel_bias.astype(F32).T[:, None, :], 0.0),
                      axis=-1)
    return pl.pallas_call(
        _table_kernel,
        out_shape=jax.ShapeDtypeStruct((2, N_KV_HEADS, 2 * BLOCK, QCOLS), F32),
        name="bias_tables",
    )(by_dist)


def _query_scale():
    scale = np.ones((1, IN_WIDTH), np.float32)
    scale[:, COL_Q:COL_K] = HEAD_DIM ** -0.5
    return scale


def _layer(layer, x, params, tab):
    B, S, D = x.shape
    T = TILE
    const = lambda *shape: pl.BlockSpec(shape, lambda b, s: (0,) * len(shape))
    of_layer = lambda *shape: pl.BlockSpec((None,) + shape, lambda b, s: (layer,) + (0,) * len(shape))
    return pl.pallas_call(
        _layer_kernel,
        out_shape=jax.ShapeDtypeStruct((B, S, D), F32),
        grid=(B, S // T),
        in_specs=[
            pl.BlockSpec((None, T, D), lambda b, s: (b, s, 0)),
            of_layer(D, IN_WIDTH),
            of_layer(len(POOL_WINDOWS), POOL_GROUP, POOL_GROUP),
            of_layer(1, POOL_WIDTH),
            of_layer(D, D),
            const(2, N_KV_HEADS, 2 * BLOCK, QCOLS),
            of_layer(N_KV_HEADS, 1, QCOLS),
            of_layer(1, D),
            of_layer(1, D),
        ],
        out_specs=pl.BlockSpec((None, T, D), lambda b, s: (b, s, 0)),
        scratch_shapes=[
            pltpu.VMEM((T, D), BF16),
            pltpu.VMEM((MAX_POOL_WINDOW + T, POOL_WIDTH), F32),
            pltpu.VMEM((N_KV_HEADS, BLOCK + T, HEAD_DIM), BF16),
            pltpu.VMEM((KV_WIDTH, BLOCK + T), BF16),
            pltpu.VMEM((N_Q_HEADS, T, HEAD_DIM), BF16),
            pltpu.VMEM((ATTN_WIDTH, T), F32),
            pltpu.VMEM((T, ATTN_WIDTH), F32),
            pltpu.VMEM((T, D), BF16),
        ],
        compiler_params=pltpu.CompilerParams(
            dimension_semantics=("arbitrary", "arbitrary"),
            vmem_limit_bytes=VMEM_LIMIT_BYTES),
        name="hybrid_layer",
    )(x, *params[:4], tab, *params[4:])


@jax.jit
def kernel(x, ln_g, ln_b, w_in, pool_w, pool_scale, sinks, w_out, rel_bias):
    D = x.shape[-1]
    sink_rows = jnp.repeat(sinks.astype(F32), BLOCK, axis=1).reshape(DEPTH, N_KV_HEADS, 1, QCOLS)
    params = ((w_in * _query_scale()).astype(BF16), pool_w.astype(BF16),
              pool_scale.reshape(DEPTH, 1, POOL_WIDTH), w_out.astype(BF16),
              sink_rows, ln_g.reshape(DEPTH, 1, D), ln_b.reshape(DEPTH, 1, D))
    tab = _bias_tables(rel_bias)
    for layer in range(DEPTH):
        x = _layer(layer, x, params, tab)
    return x
```

```python
import math

import jax
import jax.numpy as jnp
from jax import lax
from jax.experimental import pallas as pl
from jax.experimental.pallas import tpu as pltpu
import numpy as np

D_MODEL = 1024
DEPTH = 2
POOL_WIDTH = 512
POOL_WINDOWS = (2, 4, 8, 16)
POOL_GROUP = 128
MAX_POOL_WINDOW = 16
HEAD_DIM = 64
N_Q_HEADS = 8
N_KV_HEADS = 2
GQA_GROUP = 4
ATTN_WIDTH = 512
KV_WIDTH = 128
WINDOW = 128
BLOCK = 128
N_BUCKETS = 32
MAX_DISTANCE = 128
IN_WIDTH = 2304
COL_U, COL_GP, COL_Q, COL_K, COL_V, COL_GA = 0, 512, 1024, 1536, 1664, 1792
DEEPNORM_ALPHA = (2.0 * DEPTH) ** 0.25
LN_EPS = 1e-5
NEG_INF = -1e30
LOG2E = math.log2(math.e)

TILE = 1024
N_CHUNK = 256
LN_ROWS = 64
LOOKAHEAD = 2
OUT_SPLIT = 4
QCOLS = GQA_GROUP * BLOCK
VMEM_LIMIT_BYTES = 56 * 1024 * 1024

F32 = jnp.float32
BF16 = jnp.bfloat16


def _silu(v):
    return v / (1.0 + jnp.exp2(v * (-LOG2E)))


def _layer_kernel(x_ref, w_in32_ref, pw_ref, pscale_ref, w_out32_ref, tab_ref, sink_ref,
                  ln_g_ref, ln_b_ref, o_ref,
                  w_in_ref, w_out_ref, xb_scr, uext_scr, kext_scr, vext_scr, q_scr, yattn_scr, gate_scr, y_scr):
    s_idx = pl.program_id(1)
    T = TILE
    H = MAX_POOL_WINDOW

    @pl.when((pl.program_id(0) == 0) & (s_idx == 0))
    def _():
        for c0 in range(0, IN_WIDTH, N_CHUNK):
            w = w_in32_ref[:, c0:c0 + N_CHUNK]
            if COL_Q <= c0 < COL_K:
                w = w * (HEAD_DIM ** -0.5)
            w_in_ref[:, c0:c0 + N_CHUNK] = w.astype(BF16)
        for c0 in range(0, D_MODEL, N_CHUNK):
            w_out_ref[:, c0:c0 + N_CHUNK] = w_out32_ref[:, c0:c0 + N_CHUNK].astype(BF16)

    uext_scr[0:H, :] = uext_scr[T:T + H, :]
    kext_scr[:, 0:BLOCK, :] = kext_scr[:, T:T + BLOCK, :]
    vext_scr[:, 0:BLOCK] = vext_scr[:, T:T + BLOCK]

    @pl.when(s_idx == 0)
    def _():
        uext_scr[0:H, :] = jnp.zeros((H, POOL_WIDTH), F32)
        kext_scr[:, 0:BLOCK, :] = jnp.zeros((N_KV_HEADS, BLOCK, HEAD_DIM), BF16)
        vext_scr[:, 0:BLOCK] = jnp.zeros((KV_WIDTH, BLOCK), BF16)

    xb_scr[...] = x_ref[...].astype(BF16)

    def proj(col, width):
        return jnp.dot(xb_scr[...], w_in_ref[:, col:col + width], preferred_element_type=F32)

    def pool_u(p):
        c0 = p * N_CHUNK
        uext_scr[H:H + T, c0:c0 + N_CHUNK] = proj(COL_U + c0, N_CHUNK)

    def pool_mix(p):
        t_top = s_idx * T + lax.broadcasted_iota(jnp.int32, (H, POOL_GROUP), 0)
        c0 = p * N_CHUNK
        pooled = []
        for gi in range(2 * p, 2 * p + 2):
            w = POOL_WINDOWS[gi]
            g0 = gi * POOL_GROUP
            ext = uext_scr[:, g0:g0 + POOL_GROUP]
            acc, span = ext, 1
            while span < w:
                acc = acc + pltpu.roll(acc, span, axis=0)
                span *= 2
            top = acc[H:2 * H] / jnp.minimum(t_top + 1, w).astype(F32)
            mean = jnp.concatenate([top, acc[2 * H:H + T] * (1.0 / w)], axis=0)
            pooled.append((mean - ext[H:H + T]).astype(BF16))
        pooled = jnp.concatenate(pooled, axis=1)
        pw_a, pw_b = pw_ref[2 * p].astype(BF16), pw_ref[2 * p + 1].astype(BF16)
        zeros = jnp.zeros_like(pw_a)
        pw = jnp.concatenate([jnp.concatenate([pw_a, zeros], axis=1),
                              jnp.concatenate([zeros, pw_b], axis=1)], axis=0)
        mixed = jnp.dot(pooled, pw, preferred_element_type=F32)
        gate = _silu(proj(COL_GP + c0, N_CHUNK))
        y_scr[:, c0:c0 + N_CHUNK] = (mixed * pscale_ref[:, c0:c0 + N_CHUNK] * gate).astype(BF16)

    def q_proj(c):
        q = proj(COL_Q + c * N_CHUNK, N_CHUNK).astype(BF16)
        for g in range(GQA_GROUP):
            q_scr[c * GQA_GROUP + g] = q[:, g * HEAD_DIM:(g + 1) * HEAD_DIM]

    def kv_proj():
        kv = proj(COL_K, 2 * KV_WIDTH)
        for h in range(N_KV_HEADS):
            kext_scr[h, BLOCK:BLOCK + T, :] = kv[:, h * HEAD_DIM:(h + 1) * HEAD_DIM].astype(BF16)
        vext_scr[:, BLOCK:BLOCK + T] = kv[:, KV_WIDTH:2 * KV_WIDTH].T.astype(BF16)

    def scores(h, i):
        first = (s_idx == 0).astype(jnp.int32)
        r0 = i * BLOCK
        qs = q_scr[h * GQA_GROUP:(h + 1) * GQA_GROUP, r0:r0 + BLOCK, :]
        qs = qs.reshape(QCOLS, HEAD_DIM)
        kwin = kext_scr[h, r0:r0 + 2 * BLOCK, :]
        sc = lax.dot_general(kwin, qs, (((1,), (1,)), ((), ())),
                             preferred_element_type=F32)
        return sc + tab_ref[first if i == 0 else 0, h]

    def softmax_values(h, i, sc):
        sink = sink_ref[h]
        r0 = i * BLOCK
        vwin = vext_scr[h * HEAD_DIM:(h + 1) * HEAD_DIM, r0:r0 + 2 * BLOCK]
        m = jnp.maximum(jnp.max(sc, axis=0, keepdims=True), sink)
        e = jnp.exp(sc - m)
        l = jnp.sum(e, axis=0, keepdims=True) + jnp.exp(sink - m)
        o = jnp.dot(vwin, e.astype(BF16), preferred_element_type=F32)
        o = o * (1.0 / l)
        for g in range(GQA_GROUP):
            row = (h * GQA_GROUP + g) * HEAD_DIM
            yattn_scr[row:row + HEAD_DIM, r0:r0 + BLOCK] = o[:, g * BLOCK:(g + 1) * BLOCK]

    def gate_proj(c):
        c0 = c * N_CHUNK
        gate_scr[:, c0:c0 + N_CHUNK] = _silu(proj(COL_GA + c0, N_CHUNK))

    def gate_attention(c):
        c0 = c * N_CHUNK
        y_scr[:, POOL_WIDTH + c0:POOL_WIDTH + c0 + N_CHUNK] = (
            yattn_scr[c0:c0 + N_CHUNK, :].T * gate_scr[:, c0:c0 + N_CHUNK]).astype(BF16)

    def finish():
        gamma = ln_g_ref[...]
        beta = ln_b_ref[...]
        half = T // OUT_SPLIT
        for hf in range(OUT_SPLIT):
            hrows = slice(hf * half, (hf + 1) * half)
            o_ref[hrows, :] = jnp.dot(y_scr[hrows, :], w_out_ref[...], preferred_element_type=F32)
            for r in range(hf * half // LN_ROWS, (hf + 1) * half // LN_ROWS):
                rows = slice(r * LN_ROWS, (r + 1) * LN_ROWS)
                z = DEEPNORM_ALPHA * x_ref[rows, :] + o_ref[rows, :]
                mu = jnp.mean(z, axis=-1, keepdims=True)
                zc = z - mu
                var = jnp.mean(zc * zc, axis=-1, keepdims=True)
                o_ref[rows, :] = zc * lax.rsqrt(var + LN_EPS) * gamma + beta

    fillers = [lambda: pool_u(0), lambda: pool_u(1), lambda: pool_mix(0), lambda: gate_proj(0),
               lambda: pool_mix(1), lambda: gate_proj(1), lambda: gate_attention(0), lambda: None]
    blocks = [(h, i) for h in range(N_KV_HEADS) for i in range(T // BLOCK)]
    kv_proj()
    q_proj(0)
    q_proj(1)
    pending = [scores(*blocks[k]) for k in range(LOOKAHEAD)]
    for k, (h, i) in enumerate(blocks):
        softmax_values(h, i, pending.pop(0))
        if k + LOOKAHEAD < len(blocks):
            pending.append(scores(*blocks[k + LOOKAHEAD]))
        if k % 2 == 1:
            fillers[k // 2]()
    gate_attention(1)
    finish()


def _t5_bucket(dist):
    max_exact = N_BUCKETS // 2
    ratio = (np.log(np.maximum(dist, 1).astype(np.float32) / np.float32(max_exact))
             / np.float32(math.log(MAX_DISTANCE / max_exact)))
    large = max_exact + (ratio * np.float32(N_BUCKETS - max_exact)).astype(np.int32)
    large = np.minimum(large, N_BUCKETS - 1)
    return np.where(dist < max_exact, dist, large)


def _table_kernel(by_dist_ref, tab_ref):
    n_keys, lanes = 2 * BLOCK, 4 * BLOCK
    key = lax.broadcasted_iota(jnp.int32, (n_keys, BLOCK), 0)
    for hg in range(N_Q_HEADS):
        h, g = divmod(hg, GQA_GROUP)
        vec = jnp.concatenate([by_dist_ref[hg:hg + 1, :],
                               jnp.full((1, lanes - WINDOW), NEG_INF, F32)], axis=1)
        rolled = pltpu.roll(jnp.broadcast_to(vec, (n_keys, lanes)), 0, 1, stride=1, stride_axis=0)
        visible = rolled[:, BLOCK:2 * BLOCK]
        tab_ref[0, h, :, g * BLOCK:(g + 1) * BLOCK] = visible
        tab_ref[1, h, :, g * BLOCK:(g + 1) * BLOCK] = jnp.where(key >= BLOCK, visible, NEG_INF)


def _bias_tables(rel_bias):
    onehot = _t5_bucket(np.arange(WINDOW))[:, None] == np.arange(N_BUCKETS)
    by_dist = jnp.sum(jnp.where(onehot[None], rel_bias.astype(F32).T[:, None, :], 0.0),
                      axis=-1)
    return pl.pallas_call(
        _table_kernel,
        out_shape=jax.ShapeDtypeStruct((2, N_KV_HEADS, 2 * BLOCK, QCOLS), F32),
        name="bias_tables",
    )(by_dist)


def _layer(layer, x, params, tab):
    B, S, D = x.shape
    T = TILE
    const = lambda *shape: pl.BlockSpec(shape, lambda b, s: (0,) * len(shape))
    of_layer = lambda *shape, **kw: pl.BlockSpec(
        (None,) + shape, lambda b, s: (layer,) + (0,) * len(shape), **kw)
    resident = dict(pipeline_mode=pl.Buffered(1))
    return pl.pallas_call(
        _layer_kernel,
        out_shape=jax.ShapeDtypeStruct((B, S, D), F32),
        grid=(B, S // T),
        in_specs=[
            pl.BlockSpec((None, T, D), lambda b, s: (b, s, 0)),
            of_layer(D, IN_WIDTH, **resident),
            of_layer(len(POOL_WINDOWS), POOL_GROUP, POOL_GROUP),
            of_layer(1, POOL_WIDTH),
            of_layer(D, D, **resident),
            const(2, N_KV_HEADS, 2 * BLOCK, QCOLS),
            of_layer(N_KV_HEADS, 1, QCOLS),
            of_layer(1, D),
            of_layer(1, D),
        ],
        out_specs=pl.BlockSpec((None, T, D), lambda b, s: (b, s, 0)),
        scratch_shapes=[
            pltpu.VMEM((D, IN_WIDTH), BF16),
            pltpu.VMEM((D, D), BF16),
            pltpu.VMEM((T, D), BF16),
            pltpu.VMEM((MAX_POOL_WINDOW + T, POOL_WIDTH), F32),
            pltpu.VMEM((N_KV_HEADS, BLOCK + T, HEAD_DIM), BF16),
            pltpu.VMEM((KV_WIDTH, BLOCK + T), BF16),
            pltpu.VMEM((N_Q_HEADS, T, HEAD_DIM), BF16),
            pltpu.VMEM((ATTN_WIDTH, T), F32),
            pltpu.VMEM((T, ATTN_WIDTH), F32),
            pltpu.VMEM((T, D), BF16),
        ],
        compiler_params=pltpu.CompilerParams(
            dimension_semantics=("arbitrary", "arbitrary"),
            vmem_limit_bytes=VMEM_LIMIT_BYTES),
        name="hybrid_layer",
    )(x, *params[:4], tab, *params[4:])


@jax.jit
def kernel(x, ln_g, ln_b, w_in, pool_w, pool_scale, sinks, w_out, rel_bias):
    D = x.shape[-1]
    sink_rows = jnp.repeat(sinks.astype(F32), BLOCK, axis=1).reshape(DEPTH, N_KV_HEADS, 1, QCOLS)
    params = (w_in, pool_w, pool_scale.reshape(DEPTH, 1, POOL_WIDTH), w_out,
              sink_rows, ln_g.reshape(DEPTH, 1, D), ln_b.reshape(DEPTH, 1, D))
    tab = _bias_tables(rel_bias)
    for layer in range(DEPTH):
        x = _layer(layer, x, params, tab)
    return x
```

```python
import functools
import math

import jax
import jax.numpy as jnp
from jax import lax
from jax.experimental import pallas as pl
from jax.experimental.pallas import tpu as pltpu
import numpy as np

D_MODEL = 1024
DEPTH = 2
POOL_WIDTH = 512
POOL_WINDOWS = (2, 4, 8, 16)
POOL_GROUP = 128
MAX_POOL_WINDOW = 16
HEAD_DIM = 64
N_Q_HEADS = 8
N_KV_HEADS = 2
GQA_GROUP = 4
ATTN_WIDTH = 512
KV_WIDTH = 128
WINDOW = 128
BLOCK = 128
N_BUCKETS = 32
MAX_DISTANCE = 128
IN_WIDTH = 2304
COL_U, COL_GP, COL_Q, COL_K, COL_V, COL_GA = 0, 512, 1024, 1536, 1664, 1792
DEEPNORM_ALPHA = (2.0 * DEPTH) ** 0.25
LN_EPS = 1e-5
NEG_INF = -1e30
LOG2E = math.log2(math.e)

TILE = 1024
N_CHUNK = 256
LN_ROWS = 64
LOOKAHEAD = 2
OUT_SPLIT = 4
QCOLS = GQA_GROUP * BLOCK
VMEM_LIMIT_BYTES = 56 * 1024 * 1024

F32 = jnp.float32
BF16 = jnp.bfloat16


def _silu(v):
    return v / (1.0 + jnp.exp2(v * (-LOG2E)))


def _layer_kernel(layer, sinks_ref, x_ref, w_in32_ref, pw_ref, pscale_ref, w_out32_ref, tab_ref,
                  ln_g_ref, ln_b_ref, o_ref,
                  w_in_ref, w_out_ref, xb_scr, uext_scr, kext_scr, vext_scr, q_scr, yattn_scr, gate_scr, y_scr):
    s_idx = pl.program_id(1)
    T = TILE
    H = MAX_POOL_WINDOW

    @pl.when((pl.program_id(0) == 0) & (s_idx == 0))
    def _():
        for c0 in range(0, IN_WIDTH, N_CHUNK):
            w = w_in32_ref[:, c0:c0 + N_CHUNK]
            if COL_Q <= c0 < COL_K:
                w = w * (HEAD_DIM ** -0.5)
            w_in_ref[:, c0:c0 + N_CHUNK] = w.astype(BF16)
        for c0 in range(0, D_MODEL, N_CHUNK):
            w_out_ref[:, c0:c0 + N_CHUNK] = w_out32_ref[:, c0:c0 + N_CHUNK].astype(BF16)

    uext_scr[0:H, :] = uext_scr[T:T + H, :]
    kext_scr[:, 0:BLOCK, :] = kext_scr[:, T:T + BLOCK, :]
    vext_scr[:, 0:BLOCK] = vext_scr[:, T:T + BLOCK]

    @pl.when(s_idx == 0)
    def _():
        uext_scr[0:H, :] = jnp.zeros((H, POOL_WIDTH), F32)
        kext_scr[:, 0:BLOCK, :] = jnp.zeros((N_KV_HEADS, BLOCK, HEAD_DIM), BF16)
        vext_scr[:, 0:BLOCK] = jnp.zeros((KV_WIDTH, BLOCK), BF16)

    xb_scr[...] = x_ref[...].astype(BF16)

    def proj(col, width):
        return jnp.dot(xb_scr[...], w_in_ref[:, col:col + width], preferred_element_type=F32)

    def pool_u(p):
        c0 = p * N_CHUNK
        uext_scr[H:H + T, c0:c0 + N_CHUNK] = proj(COL_U + c0, N_CHUNK)

    def pool_mix(p):
        t_top = s_idx * T + lax.broadcasted_iota(jnp.int32, (H, POOL_GROUP), 0)
        c0 = p * N_CHUNK
        pooled = []
        for gi in range(2 * p, 2 * p + 2):
            w = POOL_WINDOWS[gi]
            g0 = gi * POOL_GROUP
            ext = uext_scr[:, g0:g0 + POOL_GROUP]
            acc, span = ext, 1
            while span < w:
                acc = acc + pltpu.roll(acc, span, axis=0)
                span *= 2
            top = acc[H:2 * H] / jnp.minimum(t_top + 1, w).astype(F32)
            mean = jnp.concatenate([top, acc[2 * H:H + T] * (1.0 / w)], axis=0)
            pooled.append((mean - ext[H:H + T]).astype(BF16))
        pooled = jnp.concatenate(pooled, axis=1)
        pw_a, pw_b = pw_ref[2 * p].astype(BF16), pw_ref[2 * p + 1].astype(BF16)
        zeros = jnp.zeros_like(pw_a)
        pw = jnp.concatenate([jnp.concatenate([pw_a, zeros], axis=1),
                              jnp.concatenate([zeros, pw_b], axis=1)], axis=0)
        mixed = jnp.dot(pooled, pw, preferred_element_type=F32)
        gate = _silu(proj(COL_GP + c0, N_CHUNK))
        y_scr[:, c0:c0 + N_CHUNK] = (mixed * pscale_ref[layer:layer + 1, c0:c0 + N_CHUNK] * gate).astype(BF16)

    def q_proj(c):
        q = proj(COL_Q + c * N_CHUNK, N_CHUNK).astype(BF16)
        for g in range(GQA_GROUP):
            q_scr[c * GQA_GROUP + g] = q[:, g * HEAD_DIM:(g + 1) * HEAD_DIM]

    def kv_proj():
        kv = proj(COL_K, 2 * KV_WIDTH)
        for h in range(N_KV_HEADS):
            kext_scr[h, BLOCK:BLOCK + T, :] = kv[:, h * HEAD_DIM:(h + 1) * HEAD_DIM].astype(BF16)
        vext_scr[:, BLOCK:BLOCK + T] = kv[:, KV_WIDTH:2 * KV_WIDTH].T.astype(BF16)

    def scores(h, i):
        first = (s_idx == 0).astype(jnp.int32)
        r0 = i * BLOCK
        qs = q_scr[h * GQA_GROUP:(h + 1) * GQA_GROUP, r0:r0 + BLOCK, :]
        qs = qs.reshape(QCOLS, HEAD_DIM)
        kwin = kext_scr[h, r0:r0 + 2 * BLOCK, :]
        sc = lax.dot_general(kwin, qs, (((1,), (1,)), ((), ())),
                             preferred_element_type=F32)
        return sc + tab_ref[first if i == 0 else 0, h]

    def softmax_values(h, i, sc):
        sink = jnp.concatenate(
            [jnp.full((1, BLOCK), sinks_ref[layer, h * GQA_GROUP + g], F32) for g in range(GQA_GROUP)],
            axis=1)
        r0 = i * BLOCK
        vwin = vext_scr[h * HEAD_DIM:(h + 1) * HEAD_DIM, r0:r0 + 2 * BLOCK]
        m = jnp.maximum(jnp.max(sc, axis=0, keepdims=True), sink)
        e = jnp.exp(sc - m)
        l = jnp.sum(e, axis=0, keepdims=True) + jnp.exp(sink - m)
        o = jnp.dot(vwin, e.astype(BF16), preferred_element_type=F32)
        o = o * (1.0 / l)
        for g in range(GQA_GROUP):
            row = (h * GQA_GROUP + g) * HEAD_DIM
            yattn_scr[row:row + HEAD_DIM, r0:r0 + BLOCK] = o[:, g * BLOCK:(g + 1) * BLOCK]

    def gate_proj(c):
        c0 = c * N_CHUNK
        gate_scr[:, c0:c0 + N_CHUNK] = _silu(proj(COL_GA + c0, N_CHUNK))

    def gate_attention(c):
        c0 = c * N_CHUNK
        y_scr[:, POOL_WIDTH + c0:POOL_WIDTH + c0 + N_CHUNK] = (
            yattn_scr[c0:c0 + N_CHUNK, :].T * gate_scr[:, c0:c0 + N_CHUNK]).astype(BF16)

    def finish():
        gamma = ln_g_ref[layer:layer + 1, :]
        beta = ln_b_ref[layer:layer + 1, :]
        half = T // OUT_SPLIT
        for hf in range(OUT_SPLIT):
            hrows = slice(hf * half, (hf + 1) * half)
            o_ref[hrows, :] = jnp.dot(y_scr[hrows, :], w_out_ref[...], preferred_element_type=F32)
            for r in range(hf * half // LN_ROWS, (hf + 1) * half // LN_ROWS):
                rows = slice(r * LN_ROWS, (r + 1) * LN_ROWS)
                z = DEEPNORM_ALPHA * x_ref[rows, :] + o_ref[rows, :]
                mu = jnp.mean(z, axis=-1, keepdims=True)
                zc = z - mu
                var = jnp.mean(zc * zc, axis=-1, keepdims=True)
                o_ref[rows, :] = zc * lax.rsqrt(var + LN_EPS) * gamma + beta

    fillers = [lambda: pool_u(0), lambda: pool_u(1), lambda: pool_mix(0), lambda: gate_proj(0),
               lambda: pool_mix(1), lambda: gate_proj(1), lambda: gate_attention(0), lambda: None]
    blocks = [(h, i) for h in range(N_KV_HEADS) for i in range(T // BLOCK)]
    kv_proj()
    q_proj(0)
    q_proj(1)
    pending = [scores(*blocks[k]) for k in range(LOOKAHEAD)]
    for k, (h, i) in enumerate(blocks):
        softmax_values(h, i, pending.pop(0))
        if k + LOOKAHEAD < len(blocks):
            pending.append(scores(*blocks[k + LOOKAHEAD]))
        if k % 2 == 1:
            fillers[k // 2]()
    gate_attention(1)
    finish()


def _t5_bucket(dist):
    max_exact = N_BUCKETS // 2
    ratio = (np.log(np.maximum(dist, 1).astype(np.float32) / np.float32(max_exact))
             / np.float32(math.log(MAX_DISTANCE / max_exact)))
    large = max_exact + (ratio * np.float32(N_BUCKETS - max_exact)).astype(np.int32)
    large = np.minimum(large, N_BUCKETS - 1)
    return np.where(dist < max_exact, dist, large)


def _table_kernel(by_dist_ref, tab_ref):
    n_keys, lanes = 2 * BLOCK, 4 * BLOCK
    key = lax.broadcasted_iota(jnp.int32, (n_keys, BLOCK), 0)
    for hg in range(N_Q_HEADS):
        h, g = divmod(hg, GQA_GROUP)
        vec = jnp.concatenate([by_dist_ref[hg:hg + 1, :],
                               jnp.full((1, lanes - WINDOW), NEG_INF, F32)], axis=1)
        rolled = pltpu.roll(jnp.broadcast_to(vec, (n_keys, lanes)), 0, 1, stride=1, stride_axis=0)
        visible = rolled[:, BLOCK:2 * BLOCK]
        tab_ref[0, h, :, g * BLOCK:(g + 1) * BLOCK] = visible
        tab_ref[1, h, :, g * BLOCK:(g + 1) * BLOCK] = jnp.where(key >= BLOCK, visible, NEG_INF)


def _bias_tables(rel_bias):
    onehot = _t5_bucket(np.arange(WINDOW))[:, None] == np.arange(N_BUCKETS)
    by_dist = jnp.sum(jnp.where(onehot[None], rel_bias.astype(F32).T[:, None, :], 0.0),
                      axis=-1)
    return pl.pallas_call(
        _table_kernel,
        out_shape=jax.ShapeDtypeStruct((2, N_KV_HEADS, 2 * BLOCK, QCOLS), F32),
        name="bias_tables",
    )(by_dist)


def _layer(layer, x, params, tab):
    B, S, D = x.shape
    T = TILE
    const = lambda *shape: pl.BlockSpec(shape, lambda b, s: (0,) * len(shape))
    of_layer = lambda *shape, **kw: pl.BlockSpec(
        (None,) + shape, lambda b, s: (layer,) + (0,) * len(shape), **kw)
    resident = dict(pipeline_mode=pl.Buffered(1))
    return pl.pallas_call(
        functools.partial(_layer_kernel, layer),
        out_shape=jax.ShapeDtypeStruct((B, S, D), F32),
        grid=(B, S // T),
        in_specs=[
            pl.BlockSpec(memory_space=pltpu.SMEM),
            pl.BlockSpec((None, T, D), lambda b, s: (b, s, 0)),
            of_layer(D, IN_WIDTH, **resident),
            of_layer(len(POOL_WINDOWS), POOL_GROUP, POOL_GROUP),
            const(DEPTH, POOL_WIDTH),
            of_layer(D, D, **resident),
            const(2, N_KV_HEADS, 2 * BLOCK, QCOLS),
            const(DEPTH, D),
            const(DEPTH, D),
        ],
        out_specs=pl.BlockSpec((None, T, D), lambda b, s: (b, s, 0)),
        scratch_shapes=[
            pltpu.VMEM((D, IN_WIDTH), BF16),
            pltpu.VMEM((D, D), BF16),
            pltpu.VMEM((T, D), BF16),
            pltpu.VMEM((MAX_POOL_WINDOW + T, POOL_WIDTH), F32),
            pltpu.VMEM((N_KV_HEADS, BLOCK + T, HEAD_DIM), BF16),
            pltpu.VMEM((KV_WIDTH, BLOCK + T), BF16),
            pltpu.VMEM((N_Q_HEADS, T, HEAD_DIM), BF16),
            pltpu.VMEM((ATTN_WIDTH, T), F32),
            pltpu.VMEM((T, ATTN_WIDTH), F32),
            pltpu.VMEM((T, D), BF16),
        ],
        compiler_params=pltpu.CompilerParams(
            dimension_semantics=("arbitrary", "arbitrary"),
            vmem_limit_bytes=VMEM_LIMIT_BYTES),
        name="hybrid_layer",
    )(params[0], x, *params[1:5], tab, *params[5:])


@jax.jit
def kernel(x, ln_g, ln_b, w_in, pool_w, pool_scale, sinks, w_out, rel_bias):
    params = (sinks.astype(F32), w_in, pool_w, pool_scale, w_out, ln_g, ln_b)
    tab = _bias_tables(rel_bias)
    for layer in range(DEPTH):
        x = _layer(layer, x, params, tab)
    return x
```

```python
import functools
import math

import jax
import jax.numpy as jnp
from jax import lax
from jax.experimental import pallas as pl
from jax.experimental.pallas import tpu as pltpu
import numpy as np

D_MODEL = 1024
DEPTH = 2
POOL_WIDTH = 512
POOL_WINDOWS = (2, 4, 8, 16)
POOL_GROUP = 128
MAX_POOL_WINDOW = 16
HEAD_DIM = 64
N_Q_HEADS = 8
N_KV_HEADS = 2
GQA_GROUP = 4
ATTN_WIDTH = 512
KV_WIDTH = 128
WINDOW = 128
BLOCK = 128
N_BUCKETS = 32
MAX_DISTANCE = 128
IN_WIDTH = 2304
COL_U, COL_GP, COL_Q, COL_K, COL_V, COL_GA = 0, 512, 1024, 1536, 1664, 1792
DEEPNORM_ALPHA = (2.0 * DEPTH) ** 0.25
LN_EPS = 1e-5
NEG_INF = -1e30
LOG2E = math.log2(math.e)

TILE = 1024
N_CHUNK = 256
LN_ROWS = 64
LOOKAHEAD = 5
HALF = BLOCK // 2
KEYS = BLOCK + HALF
OUT_SPLIT = 4
QCOLS = GQA_GROUP * BLOCK
VMEM_LIMIT_BYTES = 56 * 1024 * 1024

F32 = jnp.float32
BF16 = jnp.bfloat16


def _silu(v):
    return v / (1.0 + jnp.exp2(v * (-LOG2E)))


def _layer_kernel(layer, sinks_ref, x_ref, w_in32_ref, pw_ref, pscale_ref, w_out32_ref, tab_ref,
                  ln_g_ref, ln_b_ref, o_ref,
                  w_in_ref, w_out_ref, xb_scr, uext_scr, kext_scr, vext_scr, q_scr, yattn_scr, gate_scr, y_scr):
    s_idx = pl.program_id(1)
    T = TILE
    H = MAX_POOL_WINDOW

    @pl.when((pl.program_id(0) == 0) & (s_idx == 0))
    def _():
        for c0 in range(0, IN_WIDTH, N_CHUNK):
            w = w_in32_ref[:, c0:c0 + N_CHUNK]
            if COL_Q <= c0 < COL_K:
                w = w * (HEAD_DIM ** -0.5)
            w_in_ref[:, c0:c0 + N_CHUNK] = w.astype(BF16)
        for c0 in range(0, D_MODEL, N_CHUNK):
            w_out_ref[:, c0:c0 + N_CHUNK] = w_out32_ref[:, c0:c0 + N_CHUNK].astype(BF16)

    uext_scr[0:H, :] = uext_scr[T:T + H, :]
    kext_scr[:, 0:BLOCK, :] = kext_scr[:, T:T + BLOCK, :]
    vext_scr[:, 0:BLOCK] = vext_scr[:, T:T + BLOCK]

    @pl.when(s_idx == 0)
    def _():
        uext_scr[0:H, :] = jnp.zeros((H, POOL_WIDTH), F32)
        kext_scr[:, 0:BLOCK, :] = jnp.zeros((N_KV_HEADS, BLOCK, HEAD_DIM), BF16)
        vext_scr[:, 0:BLOCK] = jnp.zeros((KV_WIDTH, BLOCK), BF16)

    xb_scr[...] = x_ref[...].astype(BF16)

    def proj(col, width):
        return jnp.dot(xb_scr[...], w_in_ref[:, col:col + width], preferred_element_type=F32)

    def pool_u(p):
        c0 = p * N_CHUNK
        uext_scr[H:H + T, c0:c0 + N_CHUNK] = proj(COL_U + c0, N_CHUNK)

    def pool_mix(p):
        t_top = s_idx * T + lax.broadcasted_iota(jnp.int32, (H, POOL_GROUP), 0)
        c0 = p * N_CHUNK
        pooled = []
        for gi in range(2 * p, 2 * p + 2):
            w = POOL_WINDOWS[gi]
            g0 = gi * POOL_GROUP
            ext = uext_scr[:, g0:g0 + POOL_GROUP]
            acc, span = ext, 1
            while span < w:
                acc = acc + pltpu.roll(acc, span, axis=0)
                span *= 2
            top = acc[H:2 * H] / jnp.minimum(t_top + 1, w).astype(F32)
            mean = jnp.concatenate([top, acc[2 * H:H + T] * (1.0 / w)], axis=0)
            pooled.append((mean - ext[H:H + T]).astype(BF16))
        pooled = jnp.concatenate(pooled, axis=1)
        pw_a, pw_b = pw_ref[2 * p].astype(BF16), pw_ref[2 * p + 1].astype(BF16)
        zeros = jnp.zeros_like(pw_a)
        pw = jnp.concatenate([jnp.concatenate([pw_a, zeros], axis=1),
                              jnp.concatenate([zeros, pw_b], axis=1)], axis=0)
        mixed = jnp.dot(pooled, pw, preferred_element_type=F32)
        gate = _silu(proj(COL_GP + c0, N_CHUNK))
        y_scr[:, c0:c0 + N_CHUNK] = (mixed * pscale_ref[layer:layer + 1, c0:c0 + N_CHUNK] * gate).astype(BF16)

    def q_proj(c):
        q = proj(COL_Q + c * N_CHUNK, N_CHUNK).astype(BF16)
        for g in range(GQA_GROUP):
            q_scr[c * GQA_GROUP + g] = q[:, g * HEAD_DIM:(g + 1) * HEAD_DIM]

    def kv_proj():
        kv = proj(COL_K, 2 * KV_WIDTH)
        for h in range(N_KV_HEADS):
            kext_scr[h, BLOCK:BLOCK + T, :] = kv[:, h * HEAD_DIM:(h + 1) * HEAD_DIM].astype(BF16)
        vext_scr[:, BLOCK:BLOCK + T] = kv[:, KV_WIDTH:2 * KV_WIDTH].T.astype(BF16)

    def scores(h, i, a):
        r0 = i * BLOCK + a * HALF
        qs = q_scr[h * GQA_GROUP:(h + 1) * GQA_GROUP, r0:r0 + HALF, :]
        qs = qs.reshape(GQA_GROUP * HALF, HEAD_DIM)
        kwin = kext_scr[h, r0:r0 + KEYS, :]
        sc = lax.dot_general(kwin, qs, (((1,), (1,)), ((), ())),
                             preferred_element_type=F32)
        if i == 0:
            first = (s_idx == 0).astype(jnp.int32)
            return sc + tab_ref[first * (1 + a), h]
        return sc + tab_ref[0, h]

    def softmax_values(h, i, a, sc):
        sink = jnp.concatenate(
            [jnp.full((1, HALF), sinks_ref[layer, h * GQA_GROUP + g], F32) for g in range(GQA_GROUP)],
            axis=1)
        r0 = i * BLOCK
        vwin = vext_scr[h * HEAD_DIM:(h + 1) * HEAD_DIM, r0:r0 + 2 * BLOCK]
        m = jnp.maximum(jnp.max(sc, axis=0, keepdims=True), sink)
        e = jnp.exp(sc - m)
        l = jnp.sum(e, axis=0, keepdims=True) + jnp.exp(sink - m)
        e = e.astype(BF16)
        pad = jnp.zeros((HALF, GQA_GROUP * HALF), BF16)
        e = jnp.concatenate([e, pad] if a == 0 else [pad, e], axis=0)
        o = jnp.dot(vwin, e, preferred_element_type=F32)
        o = o * (1.0 / l)
        for g in range(GQA_GROUP):
            row = (h * GQA_GROUP + g) * HEAD_DIM
            yattn_scr[row:row + HEAD_DIM, r0 + a * HALF:r0 + (a + 1) * HALF] = o[:, g * HALF:(g + 1) * HALF]

    def gate_proj(c):
        c0 = c * N_CHUNK
        gate_scr[:, c0:c0 + N_CHUNK] = _silu(proj(COL_GA + c0, N_CHUNK))

    def gate_attention(c):
        c0 = c * N_CHUNK
        y_scr[:, POOL_WIDTH + c0:POOL_WIDTH + c0 + N_CHUNK] = (
            yattn_scr[c0:c0 + N_CHUNK, :].T * gate_scr[:, c0:c0 + N_CHUNK]).astype(BF16)

    def finish():
        gamma = ln_g_ref[layer:layer + 1, :]
        beta = ln_b_ref[layer:layer + 1, :]
        half = T // OUT_SPLIT
        for hf in range(OUT_SPLIT):
            hrows = slice(hf * half, (hf + 1) * half)
            o_ref[hrows, :] = jnp.dot(y_scr[hrows, :], w_out_ref[...], preferred_element_type=F32)
            for r in range(hf * half // LN_ROWS, (hf + 1) * half // LN_ROWS):
                rows = slice(r * LN_ROWS, (r + 1) * LN_ROWS)
                z = DEEPNORM_ALPHA * x_ref[rows, :] + o_ref[rows, :]
                mu = jnp.mean(z, axis=-1, keepdims=True)
                zc = z - mu
                var = jnp.mean(zc * zc, axis=-1, keepdims=True)
                o_ref[rows, :] = zc * lax.rsqrt(var + LN_EPS) * gamma + beta

    fillers = [lambda: pool_u(0), lambda: pool_u(1), lambda: pool_mix(0), lambda: gate_proj(0),
               lambda: pool_mix(1), lambda: gate_proj(1), lambda: gate_attention(0), lambda: None]
    blocks = [(h, i, a) for h in range(N_KV_HEADS) for i in range(T // BLOCK) for a in range(2)]
    kv_proj()
    q_proj(0)
    q_proj(1)
    pending = [scores(*blocks[k]) for k in range(LOOKAHEAD)]
    for k, (h, i, a) in enumerate(blocks):
        softmax_values(h, i, a, pending.pop(0))
        if k + LOOKAHEAD < len(blocks):
            pending.append(scores(*blocks[k + LOOKAHEAD]))
        if k % 4 == 3:
            fillers[k // 4]()
    gate_attention(1)
    finish()


def _t5_bucket(dist):
    max_exact = N_BUCKETS // 2
    ratio = (np.log(np.maximum(dist, 1).astype(np.float32) / np.float32(max_exact))
             / np.float32(math.log(MAX_DISTANCE / max_exact)))
    large = max_exact + (ratio * np.float32(N_BUCKETS - max_exact)).astype(np.int32)
    large = np.minimum(large, N_BUCKETS - 1)
    return np.where(dist < max_exact, dist, large)


def _table_kernel(by_dist_ref, tab_ref):
    lanes = 4 * BLOCK
    key = lax.broadcasted_iota(jnp.int32, (KEYS, HALF), 0)
    for hg in range(N_Q_HEADS):
        h, g = divmod(hg, GQA_GROUP)
        vec = jnp.concatenate([by_dist_ref[hg:hg + 1, :],
                               jnp.full((1, lanes - WINDOW), NEG_INF, F32)], axis=1)
        rolled = pltpu.roll(jnp.broadcast_to(vec, (KEYS, lanes)), 0, 1, stride=1, stride_axis=0)
        visible = rolled[:, BLOCK:BLOCK + HALF]
        cols = slice(g * HALF, (g + 1) * HALF)
        tab_ref[0, h, :, cols] = visible
        tab_ref[1, h, :, cols] = jnp.where(key >= BLOCK, visible, NEG_INF)
        tab_ref[2, h, :, cols] = jnp.where(key >= BLOCK - HALF, visible, NEG_INF)


def _bias_tables(rel_bias):
    onehot = _t5_bucket(np.arange(WINDOW))[:, None] == np.arange(N_BUCKETS)
    by_dist = jnp.sum(jnp.where(onehot[None], rel_bias.astype(F32).T[:, None, :], 0.0),
                      axis=-1)
    return pl.pallas_call(
        _table_kernel,
        out_shape=jax.ShapeDtypeStruct((3, N_KV_HEADS, KEYS, GQA_GROUP * HALF), F32),
        name="bias_tables",
    )(by_dist)


def _layer(layer, x, params, tab):
    B, S, D = x.shape
    T = TILE
    const = lambda *shape: pl.BlockSpec(shape, lambda b, s: (0,) * len(shape))
    of_layer = lambda *shape, **kw: pl.BlockSpec(
        (None,) + shape, lambda b, s: (layer,) + (0,) * len(shape), **kw)
    resident = dict(pipeline_mode=pl.Buffered(1))
    return pl.pallas_call(
        functools.partial(_layer_kernel, layer),
        out_shape=jax.ShapeDtypeStruct((B, S, D), F32),
        grid=(B, S // T),
        in_specs=[
            pl.BlockSpec(memory_space=pltpu.SMEM),
            pl.BlockSpec((None, T, D), lambda b, s: (b, s, 0)),
            of_layer(D, IN_WIDTH, **resident),
            of_layer(len(POOL_WINDOWS), POOL_GROUP, POOL_GROUP),
            const(DEPTH, POOL_WIDTH),
            of_layer(D, D, **resident),
            const(3, N_KV_HEADS, KEYS, GQA_GROUP * HALF),
            const(DEPTH, D),
            const(DEPTH, D),
        ],
        out_specs=pl.BlockSpec((None, T, D), lambda b, s: (b, s, 0)),
        scratch_shapes=[
            pltpu.VMEM((D, IN_WIDTH), BF16),
            pltpu.VMEM((D, D), BF16),
            pltpu.VMEM((T, D), BF16),
            pltpu.VMEM((MAX_POOL_WINDOW + T, POOL_WIDTH), F32),
            pltpu.VMEM((N_KV_HEADS, BLOCK + T, HEAD_DIM), BF16),
            pltpu.VMEM((KV_WIDTH, BLOCK + T), BF16),
            pltpu.VMEM((N_Q_HEADS, T, HEAD_DIM), BF16),
            pltpu.VMEM((ATTN_WIDTH, T), F32),
            pltpu.VMEM((T, ATTN_WIDTH), F32),
            pltpu.VMEM((T, D), BF16),
        ],
        compiler_params=pltpu.CompilerParams(
            dimension_semantics=("arbitrary", "arbitrary"),
            vmem_limit_bytes=VMEM_LIMIT_BYTES),
        name="hybrid_layer",
    )(params[0], x, *params[1:5], tab, *params[5:])


@jax.jit
def kernel(x, ln_g, ln_b, w_in, pool_w, pool_scale, sinks, w_out, rel_bias):
    params = (sinks.astype(F32), w_in, pool_w, pool_scale, w_out, ln_g, ln_b)
    tab = _bias_tables(rel_bias)
    for layer in range(DEPTH):
        x = _layer(layer, x, params, tab)
    return x
```

```python
import functools
import math

import jax
import jax.numpy as jnp
from jax import lax
from jax.experimental import pallas as pl
from jax.experimental.pallas import tpu as pltpu
import numpy as np

D_MODEL = 1024
DEPTH = 2
POOL_WIDTH = 512
POOL_WINDOWS = (2, 4, 8, 16)
POOL_GROUP = 128
MAX_POOL_WINDOW = 16
HEAD_DIM = 64
N_Q_HEADS = 8
N_KV_HEADS = 2
GQA_GROUP = 4
ATTN_WIDTH = 512
KV_WIDTH = 128
WINDOW = 128
BLOCK = 128
N_BUCKETS = 32
MAX_DISTANCE = 128
IN_WIDTH = 2304
COL_U, COL_GP, COL_Q, COL_K, COL_V, COL_GA = 0, 512, 1024, 1536, 1664, 1792
DEEPNORM_ALPHA = (2.0 * DEPTH) ** 0.25
LN_EPS = 1e-5
NEG_INF = -1e30
LOG2E = math.log2(math.e)

TILE = 1024
N_CHUNK = 256
LN_ROWS = 64
LOOKAHEAD = 5
HALF = BLOCK // 2
KEYS = BLOCK + HALF
OUT_SPLIT = 4
VMEM_LIMIT_BYTES = 56 * 1024 * 1024

F32 = jnp.float32
BF16 = jnp.bfloat16


def _silu(v):
    return v / (1.0 + jnp.exp2(v * (-LOG2E)))


def _layer_kernel(layer, sinks_ref, x_ref, w_in32_ref, pw_ref, pscale_ref, w_out32_ref, tab_ref,
                  ln_g_ref, ln_b_ref, o_ref,
                  w_in_ref, w_out_ref, xb_scr, uext_scr, kext_scr, vext_scr, q_scr, yattn_scr, gate_scr, y_scr):
    s_idx = pl.program_id(1)
    T = TILE
    H = MAX_POOL_WINDOW

    @pl.when((pl.program_id(0) == 0) & (s_idx == 0))
    def _():
        for c0 in range(0, IN_WIDTH, N_CHUNK):
            w = w_in32_ref[:, c0:c0 + N_CHUNK]
            if COL_Q <= c0 < COL_K:
                w = w * (HEAD_DIM ** -0.5)
            w_in_ref[:, c0:c0 + N_CHUNK] = w.astype(BF16)
        for c0 in range(0, D_MODEL, N_CHUNK):
            w_out_ref[:, c0:c0 + N_CHUNK] = w_out32_ref[:, c0:c0 + N_CHUNK].astype(BF16)

    uext_scr[0:H, :] = uext_scr[T:T + H, :]
    kext_scr[:, 0:BLOCK, :] = kext_scr[:, T:T + BLOCK, :]
    vext_scr[:, 0:BLOCK] = vext_scr[:, T:T + BLOCK]

    @pl.when(s_idx == 0)
    def _():
        uext_scr[0:H, :] = jnp.zeros((H, POOL_WIDTH), F32)
        kext_scr[:, 0:BLOCK, :] = jnp.zeros((N_KV_HEADS, BLOCK, HEAD_DIM), BF16)
        vext_scr[:, 0:BLOCK] = jnp.zeros((KV_WIDTH, BLOCK), BF16)

    xb_scr[...] = x_ref[...].astype(BF16)

    def proj(col, width):
        return jnp.dot(xb_scr[...], w_in_ref[:, col:col + width], preferred_element_type=F32)

    def pool_u(p):
        c0 = p * N_CHUNK
        uext_scr[H:H + T, c0:c0 + N_CHUNK] = proj(COL_U + c0, N_CHUNK)

    def pool_mix(p):
        t_top = s_idx * T + lax.broadcasted_iota(jnp.int32, (H, POOL_GROUP), 0)
        c0 = p * N_CHUNK
        pooled = []
        for gi in range(2 * p, 2 * p + 2):
            w = POOL_WINDOWS[gi]
            g0 = gi * POOL_GROUP
            ext = uext_scr[:, g0:g0 + POOL_GROUP]
            acc, span = ext, 1
            while span < w:
                acc = acc + pltpu.roll(acc, span, axis=0)
                span *= 2
            top = acc[H:2 * H] / jnp.minimum(t_top + 1, w).astype(F32)
            mean = jnp.concatenate([top, acc[2 * H:H + T] * (1.0 / w)], axis=0)
            pooled.append((mean - ext[H:H + T]).astype(BF16))
        pooled = jnp.concatenate(pooled, axis=1)
        pw_a, pw_b = pw_ref[2 * p].astype(BF16), pw_ref[2 * p + 1].astype(BF16)
        zeros = jnp.zeros_like(pw_a)
        pw = jnp.concatenate([jnp.concatenate([pw_a, zeros], axis=1),
                              jnp.concatenate([zeros, pw_b], axis=1)], axis=0)
        mixed = jnp.dot(pooled, pw, preferred_element_type=F32)
        gate = _silu(proj(COL_GP + c0, N_CHUNK))
        y_scr[:, c0:c0 + N_CHUNK] = (mixed * pscale_ref[layer:layer + 1, c0:c0 + N_CHUNK] * gate).astype(BF16)

    def q_proj(c):
        q = proj(COL_Q + c * N_CHUNK, N_CHUNK).astype(BF16)
        for g in range(GQA_GROUP):
            q_scr[c * GQA_GROUP + g] = q[:, g * HEAD_DIM:(g + 1) * HEAD_DIM]

    def kv_proj():
        kv = proj(COL_K, 2 * KV_WIDTH)
        for h in range(N_KV_HEADS):
            kext_scr[h, BLOCK:BLOCK + T, :] = kv[:, h * HEAD_DIM:(h + 1) * HEAD_DIM].astype(BF16)
        vext_scr[:, BLOCK:BLOCK + T] = kv[:, KV_WIDTH:2 * KV_WIDTH].T.astype(BF16)

    def scores(h, i, a):
        r0 = i * BLOCK + a * HALF
        qs = q_scr[h * GQA_GROUP:(h + 1) * GQA_GROUP, r0:r0 + HALF, :]
        qs = qs.reshape(GQA_GROUP * HALF, HEAD_DIM)
        kwin = kext_scr[h, r0:r0 + KEYS, :]
        sc = lax.dot_general(kwin, qs, (((1,), (1,)), ((), ())),
                             preferred_element_type=F32)
        if i == 0:
            first = (s_idx == 0).astype(jnp.int32)
            return sc + tab_ref[first * (1 + a), h]
        return sc + tab_ref[0, h]

    def softmax_values(h, i, a, sc):
        sink = jnp.concatenate(
            [jnp.full((1, HALF), sinks_ref[layer, h * GQA_GROUP + g], F32) for g in range(GQA_GROUP)],
            axis=1)
        r0 = i * BLOCK
        vwin = vext_scr[h * HEAD_DIM:(h + 1) * HEAD_DIM, r0:r0 + 2 * BLOCK]
        m = jnp.maximum(jnp.max(sc, axis=0, keepdims=True), sink)
        e = jnp.exp(sc - m)
        l = jnp.sum(e, axis=0, keepdims=True) + jnp.exp(sink - m)
        e = e.astype(BF16)
        pad = jnp.zeros((HALF, GQA_GROUP * HALF), BF16)
        e = jnp.concatenate([e, pad] if a == 0 else [pad, e], axis=0)
        o = jnp.dot(vwin, e, preferred_element_type=F32)
        o = o * (1.0 / l)
        for g in range(GQA_GROUP):
            row = (h * GQA_GROUP + g) * HEAD_DIM
            yattn_scr[row:row + HEAD_DIM, r0 + a * HALF:r0 + (a + 1) * HALF] = o[:, g * HALF:(g + 1) * HALF]

    def gate_proj(c):
        c0 = c * N_CHUNK
        gate_scr[:, c0:c0 + N_CHUNK] = _silu(proj(COL_GA + c0, N_CHUNK))

    def gate_attention(c):
        c0 = c * N_CHUNK
        y_scr[:, POOL_WIDTH + c0:POOL_WIDTH + c0 + N_CHUNK] = (
            yattn_scr[c0:c0 + N_CHUNK, :].T * gate_scr[:, c0:c0 + N_CHUNK]).astype(BF16)

    def finish():
        gamma = ln_g_ref[layer:layer + 1, :]
        beta = ln_b_ref[layer:layer + 1, :]
        half = T // OUT_SPLIT
        for hf in range(OUT_SPLIT):
            hrows = slice(hf * half, (hf + 1) * half)
            o_ref[hrows, :] = jnp.dot(y_scr[hrows, :], w_out_ref[...], preferred_element_type=F32)
            for r in range(hf * half // LN_ROWS, (hf + 1) * half // LN_ROWS):
                rows = slice(r * LN_ROWS, (r + 1) * LN_ROWS)
                z = DEEPNORM_ALPHA * x_ref[rows, :] + o_ref[rows, :]
                mu = jnp.mean(z, axis=-1, keepdims=True)
                zc = z - mu
                var = jnp.mean(zc * zc, axis=-1, keepdims=True)
                o_ref[rows, :] = zc * lax.rsqrt(var + LN_EPS) * gamma + beta

    fillers = [lambda: pool_u(0), lambda: pool_u(1), lambda: pool_mix(0), lambda: gate_proj(0),
               lambda: pool_mix(1), lambda: gate_proj(1), lambda: gate_attention(0), lambda: None]
    blocks = [(h, i, a) for h in range(N_KV_HEADS) for i in range(T // BLOCK) for a in range(2)]
    kv_proj()
    q_proj(0)
    q_proj(1)
    pending = [scores(*blocks[k]) for k in range(LOOKAHEAD)]
    for k, (h, i, a) in enumerate(blocks):
        softmax_values(h, i, a, pending.pop(0))
        if k + LOOKAHEAD < len(blocks):
            pending.append(scores(*blocks[k + LOOKAHEAD]))
        if k % 4 == 3:
            fillers[k // 4]()
    gate_attention(1)
    finish()


def _t5_bucket(dist):
    max_exact = N_BUCKETS // 2
    ratio = (np.log(np.maximum(dist, 1).astype(np.float32) / np.float32(max_exact))
             / np.float32(math.log(MAX_DISTANCE / max_exact)))
    large = max_exact + (ratio * np.float32(N_BUCKETS - max_exact)).astype(np.int32)
    large = np.minimum(large, N_BUCKETS - 1)
    return np.where(dist < max_exact, dist, large)


def _table_kernel(by_dist_ref, tab_ref):
    lanes = 4 * BLOCK
    key = lax.broadcasted_iota(jnp.int32, (KEYS, HALF), 0)
    for hg in range(N_Q_HEADS):
        h, g = divmod(hg, GQA_GROUP)
        vec = jnp.concatenate([by_dist_ref[hg:hg + 1, :],
                               jnp.full((1, lanes - WINDOW), NEG_INF, F32)], axis=1)
        rolled = pltpu.roll(jnp.broadcast_to(vec, (KEYS, lanes)), 0, 1, stride=1, stride_axis=0)
        visible = rolled[:, BLOCK:BLOCK + HALF]
        cols = slice(g * HALF, (g + 1) * HALF)
        tab_ref[0, h, :, cols] = visible
        tab_ref[1, h, :, cols] = jnp.where(key >= BLOCK, visible, NEG_INF)
        tab_ref[2, h, :, cols] = jnp.where(key >= BLOCK - HALF, visible, NEG_INF)


def _bias_tables(rel_bias):
    onehot = _t5_bucket(np.arange(WINDOW))[:, None] == np.arange(N_BUCKETS)
    by_dist = jnp.sum(jnp.where(onehot[None], rel_bias.astype(F32).T[:, None, :], 0.0),
                      axis=-1)
    return pl.pallas_call(
        _table_kernel,
        out_shape=jax.ShapeDtypeStruct((3, N_KV_HEADS, KEYS, GQA_GROUP * HALF), F32),
        name="bias_tables",
    )(by_dist)


def _layer(layer, x, params, tab):
    B, S, D = x.shape
    T = TILE
    const = lambda *shape: pl.BlockSpec(shape, lambda b, s: (0,) * len(shape))
    of_layer = lambda *shape, **kw: pl.BlockSpec(
        (None,) + shape, lambda b, s: (layer,) + (0,) * len(shape), **kw)
    resident = dict(pipeline_mode=pl.Buffered(1))
    return pl.pallas_call(
        functools.partial(_layer_kernel, layer),
        out_shape=jax.ShapeDtypeStruct((B, S, D), F32),
        grid=(B, S // T),
        in_specs=[
            pl.BlockSpec(memory_space=pltpu.SMEM),
            pl.BlockSpec((None, T, D), lambda b, s: (b, s, 0)),
            of_layer(D, IN_WIDTH, **resident),
            of_layer(len(POOL_WINDOWS), POOL_GROUP, POOL_GROUP),
            const(DEPTH, POOL_WIDTH),
            of_layer(D, D, **resident),
            const(3, N_KV_HEADS, KEYS, GQA_GROUP * HALF),
            const(DEPTH, D),
            const(DEPTH, D),
        ],
        out_specs=pl.BlockSpec((None, T, D), lambda b, s: (b, s, 0)),
        scratch_shapes=[
            pltpu.VMEM((D, IN_WIDTH), BF16),
            pltpu.VMEM((D, D), BF16),
            pltpu.VMEM((T, D), BF16),
            pltpu.VMEM((MAX_POOL_WINDOW + T, POOL_WIDTH), F32),
            pltpu.VMEM((N_KV_HEADS, BLOCK + T, HEAD_DIM), BF16),
            pltpu.VMEM((KV_WIDTH, BLOCK + T), BF16),
            pltpu.VMEM((N_Q_HEADS, T, HEAD_DIM), BF16),
            pltpu.VMEM((ATTN_WIDTH, T), F32),
            pltpu.VMEM((T, ATTN_WIDTH), F32),
            pltpu.VMEM((T, D), BF16),
        ],
        compiler_params=pltpu.CompilerParams(
            dimension_semantics=("arbitrary", "arbitrary"),
            vmem_limit_bytes=VMEM_LIMIT_BYTES),
        name="hybrid_layer",
    )(params[0], x, *params[1:5], tab, *params[5:])


@jax.jit
def kernel(x, ln_g, ln_b, w_in, pool_w, pool_scale, sinks, w_out, rel_bias):
    params = (sinks.astype(F32), w_in, pool_w, pool_scale, w_out, ln_g, ln_b)
    tab = _bias_tables(rel_bias)
    for layer in range(DEPTH):
        x = _layer(layer, x, params, tab)
    return x
```

```python
import functools
import math

import jax
import jax.numpy as jnp
from jax import lax
from jax.experimental import pallas as pl
from jax.experimental.pallas import tpu as pltpu
import numpy as np

D_MODEL = 1024
DEPTH = 2
POOL_WIDTH = 512
POOL_WINDOWS = (2, 4, 8, 16)
POOL_GROUP = 128
MAX_POOL_WINDOW = 16
HEAD_DIM = 64
N_Q_HEADS = 8
N_KV_HEADS = 2
GQA_GROUP = 4
ATTN_WIDTH = 512
KV_WIDTH = 128
WINDOW = 128
BLOCK = 128
N_BUCKETS = 32
MAX_DISTANCE = 128
IN_WIDTH = 2304
COL_U, COL_GP, COL_Q, COL_K, COL_V, COL_GA = 0, 512, 1024, 1536, 1664, 1792
DEEPNORM_ALPHA = (2.0 * DEPTH) ** 0.25
LN_EPS = 1e-5
NEG_INF = -1e30
LOG2E = math.log2(math.e)

TILE = 1024
N_CHUNK = 256
LN_ROWS = 64
LOOKAHEAD = 5
HALF = BLOCK // 2
KEYS = BLOCK + HALF
OUT_SPLIT = 4
OUT_TILE = 1024
N_OUT_TILES = 16 * 2048 // OUT_TILE
VMEM_LIMIT_BYTES = 56 * 1024 * 1024

F32 = jnp.float32
BF16 = jnp.bfloat16


def _silu(v):
    return v / (1.0 + jnp.exp2(v * (-LOG2E)))


def _mixer_kernel(layer, sinks_ref, x_ref, w_in32_ref, pw_ref, pscale_ref, tab_ref, y_ref,
                  w_in_ref, xb_scr, uext_scr, kext_scr, vext_scr, q_scr, yattn_scr, gate_scr):
    s_idx = pl.program_id(1)
    T = TILE
    H = MAX_POOL_WINDOW

    @pl.when((pl.program_id(0) == 0) & (s_idx == 0))
    def _():
        for c0 in range(0, IN_WIDTH, N_CHUNK):
            w = w_in32_ref[:, c0:c0 + N_CHUNK]
            if COL_Q <= c0 < COL_K:
                w = w * (HEAD_DIM ** -0.5)
            w_in_ref[:, c0:c0 + N_CHUNK] = w.astype(BF16)

    uext_scr[0:H, :] = uext_scr[T:T + H, :]
    kext_scr[:, 0:BLOCK, :] = kext_scr[:, T:T + BLOCK, :]
    vext_scr[:, 0:BLOCK] = vext_scr[:, T:T + BLOCK]

    @pl.when(s_idx == 0)
    def _():
        uext_scr[0:H, :] = jnp.zeros((H, POOL_WIDTH), F32)
        kext_scr[:, 0:BLOCK, :] = jnp.zeros((N_KV_HEADS, BLOCK, HEAD_DIM), BF16)
        vext_scr[:, 0:BLOCK] = jnp.zeros((KV_WIDTH, BLOCK), BF16)

    xb_scr[...] = x_ref[...].astype(BF16)

    def proj(col, width):
        return jnp.dot(xb_scr[...], w_in_ref[:, col:col + width], preferred_element_type=F32)

    def pool_u(p):
        c0 = p * N_CHUNK
        uext_scr[H:H + T, c0:c0 + N_CHUNK] = proj(COL_U + c0, N_CHUNK)

    def pool_mix(p):
        t_top = s_idx * T + lax.broadcasted_iota(jnp.int32, (H, POOL_GROUP), 0)
        c0 = p * N_CHUNK
        pooled = []
        for gi in range(2 * p, 2 * p + 2):
            w = POOL_WINDOWS[gi]
            g0 = gi * POOL_GROUP
            ext = uext_scr[:, g0:g0 + POOL_GROUP]
            acc, span = ext, 1
            while span < w:
                acc = acc + pltpu.roll(acc, span, axis=0)
                span *= 2
            top = acc[H:2 * H] / jnp.minimum(t_top + 1, w).astype(F32)
            mean = jnp.concatenate([top, acc[2 * H:H + T] * (1.0 / w)], axis=0)
            pooled.append((mean - ext[H:H + T]).astype(BF16))
        pooled = jnp.concatenate(pooled, axis=1)
        pw_a, pw_b = pw_ref[2 * p].astype(BF16), pw_ref[2 * p + 1].astype(BF16)
        zeros = jnp.zeros_like(pw_a)
        pw = jnp.concatenate([jnp.concatenate([pw_a, zeros], axis=1),
                              jnp.concatenate([zeros, pw_b], axis=1)], axis=0)
        mixed = jnp.dot(pooled, pw, preferred_element_type=F32)
        gate = _silu(proj(COL_GP + c0, N_CHUNK))
        y_ref[:, c0:c0 + N_CHUNK] = (mixed * pscale_ref[layer:layer + 1, c0:c0 + N_CHUNK] * gate).astype(BF16)

    def q_proj(c):
        q = proj(COL_Q + c * N_CHUNK, N_CHUNK).astype(BF16)
        for g in range(GQA_GROUP):
            q_scr[c * GQA_GROUP + g] = q[:, g * HEAD_DIM:(g + 1) * HEAD_DIM]

    def kv_proj():
        kv = proj(COL_K, 2 * KV_WIDTH)
        for h in range(N_KV_HEADS):
            kext_scr[h, BLOCK:BLOCK + T, :] = kv[:, h * HEAD_DIM:(h + 1) * HEAD_DIM].astype(BF16)
        vext_scr[:, BLOCK:BLOCK + T] = kv[:, KV_WIDTH:2 * KV_WIDTH].T.astype(BF16)

    def scores(h, i, a):
        r0 = i * BLOCK + a * HALF
        qs = q_scr[h * GQA_GROUP:(h + 1) * GQA_GROUP, r0:r0 + HALF, :]
        qs = qs.reshape(GQA_GROUP * HALF, HEAD_DIM)
        kwin = kext_scr[h, r0:r0 + KEYS, :]
        sc = lax.dot_general(kwin, qs, (((1,), (1,)), ((), ())),
                             preferred_element_type=F32)
        if i == 0:
            first = (s_idx == 0).astype(jnp.int32)
            return sc + tab_ref[first * (1 + a), h]
        return sc + tab_ref[0, h]

    def softmax_values(h, i, a, sc):
        sink = jnp.concatenate(
            [jnp.full((1, HALF), sinks_ref[layer, h * GQA_GROUP + g], F32) for g in range(GQA_GROUP)],
            axis=1)
        r0 = i * BLOCK
        vwin = vext_scr[h * HEAD_DIM:(h + 1) * HEAD_DIM, r0:r0 + 2 * BLOCK]
        m = jnp.maximum(jnp.max(sc, axis=0, keepdims=True), sink)
        e = jnp.exp(sc - m)
        l = jnp.sum(e, axis=0, keepdims=True) + jnp.exp(sink - m)
        e = e.astype(BF16)
        pad = jnp.zeros((HALF, GQA_GROUP * HALF), BF16)
        e = jnp.concatenate([e, pad] if a == 0 else [pad, e], axis=0)
        o = jnp.dot(vwin, e, preferred_element_type=F32)
        o = o * (1.0 / l)
        for g in range(GQA_GROUP):
            row = (h * GQA_GROUP + g) * HEAD_DIM
            yattn_scr[row:row + HEAD_DIM, r0 + a * HALF:r0 + (a + 1) * HALF] = o[:, g * HALF:(g + 1) * HALF]

    def gate_proj(c):
        c0 = c * N_CHUNK
        gate_scr[:, c0:c0 + N_CHUNK] = _silu(proj(COL_GA + c0, N_CHUNK))

    def gate_attention(c):
        c0 = c * N_CHUNK
        y_ref[:, POOL_WIDTH + c0:POOL_WIDTH + c0 + N_CHUNK] = (
            yattn_scr[c0:c0 + N_CHUNK, :].T * gate_scr[:, c0:c0 + N_CHUNK]).astype(BF16)

    fillers = [lambda: pool_u(0), lambda: pool_u(1), lambda: pool_mix(0), lambda: gate_proj(0),
               lambda: pool_mix(1), lambda: gate_proj(1), lambda: gate_attention(0), lambda: None]
    blocks = [(h, i, a) for h in range(N_KV_HEADS) for i in range(T // BLOCK) for a in range(2)]
    kv_proj()
    q_proj(0)
    q_proj(1)
    pending = [scores(*blocks[k]) for k in range(LOOKAHEAD)]
    for k, (h, i, a) in enumerate(blocks):
        softmax_values(h, i, a, pending.pop(0))
        if k + LOOKAHEAD < len(blocks):
            pending.append(scores(*blocks[k + LOOKAHEAD]))
        if k % 4 == 3:
            fillers[k // 4]()
    gate_attention(1)


def _output_kernel(layer, x_ref, y_ref, w_out32_ref, ln_g_ref, ln_b_ref, o_ref, w_out_ref, z_scr):
    step = pl.program_id(0)
    last = pl.num_programs(0) - 1
    T = OUT_TILE
    gamma = ln_g_ref[layer:layer + 1, :]
    beta = ln_b_ref[layer:layer + 1, :]

    @pl.when(step == 0)
    def _():
        for c0 in range(0, D_MODEL, N_CHUNK):
            w_out_ref[:, c0:c0 + N_CHUNK] = w_out32_ref[:, c0:c0 + N_CHUNK].astype(BF16)
        z_scr[1] = jnp.zeros((T, D_MODEL), F32)

    def layer_norm(slot, lo=0, hi=T // LN_ROWS):
        for r in range(lo, hi):
            rows = slice(r * LN_ROWS, (r + 1) * LN_ROWS)
            z = z_scr[slot, rows, :]
            mu = jnp.mean(z, axis=-1, keepdims=True)
            zc = z - mu
            var = jnp.mean(zc * zc, axis=-1, keepdims=True)
            o_ref[rows, :] = zc * lax.rsqrt(var + LN_EPS) * gamma + beta

    def residual_sum(slot, rows=slice(None)):
        z_scr[slot, rows, :] = DEEPNORM_ALPHA * x_ref[rows, :] + jnp.dot(
            y_ref[rows, :], w_out_ref[...], preferred_element_type=F32)

    for slot in range(2):
        @pl.when((step < last) & (lax.rem(step, 2) == slot))
        def _(slot=slot):
            for g in range(OUT_SPLIT):
                residual_sum(slot, slice(g * (T // OUT_SPLIT), (g + 1) * (T // OUT_SPLIT)))
                layer_norm(1 - slot, g * (T // LN_ROWS // OUT_SPLIT), (g + 1) * (T // LN_ROWS // OUT_SPLIT))

    @pl.when(step == last)
    def _():
        layer_norm((N_OUT_TILES - 1) % 2)


def _t5_bucket(dist):
    max_exact = N_BUCKETS // 2
    ratio = (np.log(np.maximum(dist, 1).astype(np.float32) / np.float32(max_exact))
             / np.float32(math.log(MAX_DISTANCE / max_exact)))
    large = max_exact + (ratio * np.float32(N_BUCKETS - max_exact)).astype(np.int32)
    large = np.minimum(large, N_BUCKETS - 1)
    return np.where(dist < max_exact, dist, large)


def _table_kernel(by_dist_ref, tab_ref):
    lanes = 4 * BLOCK
    key = lax.broadcasted_iota(jnp.int32, (KEYS, HALF), 0)
    for hg in range(N_Q_HEADS):
        h, g = divmod(hg, GQA_GROUP)
        vec = jnp.concatenate([by_dist_ref[hg:hg + 1, :],
                               jnp.full((1, lanes - WINDOW), NEG_INF, F32)], axis=1)
        rolled = pltpu.roll(jnp.broadcast_to(vec, (KEYS, lanes)), 0, 1, stride=1, stride_axis=0)
        visible = rolled[:, BLOCK:BLOCK + HALF]
        cols = slice(g * HALF, (g + 1) * HALF)
        tab_ref[0, h, :, cols] = visible
        tab_ref[1, h, :, cols] = jnp.where(key >= BLOCK, visible, NEG_INF)
        tab_ref[2, h, :, cols] = jnp.where(key >= BLOCK - HALF, visible, NEG_INF)


def _bias_tables(rel_bias):
    onehot = _t5_bucket(np.arange(WINDOW))[:, None] == np.arange(N_BUCKETS)
    by_dist = jnp.sum(jnp.where(onehot[None], rel_bias.astype(F32).T[:, None, :], 0.0),
                      axis=-1)
    return pl.pallas_call(
        _table_kernel,
        out_shape=jax.ShapeDtypeStruct((3, N_KV_HEADS, KEYS, GQA_GROUP * HALF), F32),
        name="bias_tables",
    )(by_dist)


def _layer(layer, x, params, tab):
    B, S, D = x.shape
    T = TILE
    sinks, w_in, pool_w, pool_scale, w_out, ln_g, ln_b = params
    const = lambda *shape: pl.BlockSpec(shape, lambda b, s: (0,) * len(shape))
    of_layer = lambda *shape, **kw: pl.BlockSpec(
        (None,) + shape, lambda b, s: (layer,) + (0,) * len(shape), **kw)
    resident = dict(pipeline_mode=pl.Buffered(1))
    y = pl.pallas_call(
        functools.partial(_mixer_kernel, layer),
        out_shape=jax.ShapeDtypeStruct((B, S, D), BF16),
        grid=(B, S // T),
        in_specs=[
            pl.BlockSpec(memory_space=pltpu.SMEM),
            pl.BlockSpec((None, T, D), lambda b, s: (b, s, 0)),
            of_layer(D, IN_WIDTH, **resident),
            of_layer(len(POOL_WINDOWS), POOL_GROUP, POOL_GROUP),
            const(DEPTH, POOL_WIDTH),
            const(3, N_KV_HEADS, KEYS, GQA_GROUP * HALF),
        ],
        out_specs=pl.BlockSpec((None, T, D), lambda b, s: (b, s, 0)),
        scratch_shapes=[
            pltpu.VMEM((D, IN_WIDTH), BF16),
            pltpu.VMEM((T, D), BF16),
            pltpu.VMEM((MAX_POOL_WINDOW + T, POOL_WIDTH), F32),
            pltpu.VMEM((N_KV_HEADS, BLOCK + T, HEAD_DIM), BF16),
            pltpu.VMEM((KV_WIDTH, BLOCK + T), BF16),
            pltpu.VMEM((N_Q_HEADS, T, HEAD_DIM), BF16),
            pltpu.VMEM((ATTN_WIDTH, T), F32),
            pltpu.VMEM((T, ATTN_WIDTH), F32),
        ],
        compiler_params=pltpu.CompilerParams(
            dimension_semantics=("arbitrary", "arbitrary"),
            vmem_limit_bytes=VMEM_LIMIT_BYTES),
        name="hybrid_mixer",
    )(sinks, x, w_in, pool_w, pool_scale, tab)

    rows = B * S
    n_tiles = rows // OUT_TILE
    assert n_tiles == N_OUT_TILES
    tile = lambda n: (n, 0)
    whole = lambda *shape: pl.BlockSpec(shape, lambda n: (0,) * len(shape))
    out = pl.pallas_call(
        functools.partial(_output_kernel, layer),
        out_shape=jax.ShapeDtypeStruct((rows, D), F32),
        grid=(n_tiles + 1,),
        in_specs=[
            pl.BlockSpec((OUT_TILE, D), lambda n: tile(jnp.minimum(n, n_tiles - 1))),
            pl.BlockSpec((OUT_TILE, D), lambda n: tile(jnp.minimum(n, n_tiles - 1))),
            pl.BlockSpec((None, D, D), lambda n: (layer, 0, 0), **resident),
            whole(DEPTH, D),
            whole(DEPTH, D),
        ],
        out_specs=pl.BlockSpec((OUT_TILE, D), lambda n: tile(jnp.maximum(n - 1, 0))),
        scratch_shapes=[
            pltpu.VMEM((D, D), BF16),
            pltpu.VMEM((2, OUT_TILE, D), F32),
        ],
        compiler_params=pltpu.CompilerParams(
            dimension_semantics=("arbitrary",),
            vmem_limit_bytes=VMEM_LIMIT_BYTES),
        name="hybrid_output",
    )(x.reshape(rows, D), y.reshape(rows, D), w_out, ln_g, ln_b)
    return out.reshape(B, S, D)


@jax.jit
def kernel(x, ln_g, ln_b, w_in, pool_w, pool_scale, sinks, w_out, rel_bias):
    params = (sinks.astype(F32), w_in, pool_w, pool_scale, w_out, ln_g, ln_b)
    tab = _bias_tables(rel_bias)
    for layer in range(DEPTH):
        x = _layer(layer, x, params, tab)
    return x
```

```python
import functools
import math

import jax
import jax.numpy as jnp
from jax import lax
from jax.experimental import pallas as pl
from jax.experimental.pallas import tpu as pltpu
import numpy as np

D_MODEL = 1024
DEPTH = 2
POOL_WIDTH = 512
POOL_WINDOWS = (2, 4, 8, 16)
POOL_GROUP = 128
MAX_POOL_WINDOW = 16
HEAD_DIM = 64
N_Q_HEADS = 8
N_KV_HEADS = 2
GQA_GROUP = 4
ATTN_WIDTH = 512
KV_WIDTH = 128
WINDOW = 128
BLOCK = 128
N_BUCKETS = 32
MAX_DISTANCE = 128
IN_WIDTH = 2304
COL_U, COL_GP, COL_Q, COL_K, COL_V, COL_GA = 0, 512, 1024, 1536, 1664, 1792
DEEPNORM_ALPHA = (2.0 * DEPTH) ** 0.25
LN_EPS = 1e-5
NEG_INF = -1e30
LOG2E = math.log2(math.e)

TILE = 1024
N_CHUNK = 256
LN_ROWS = 64
LOOKAHEAD = 5
HALF = BLOCK // 2
KEYS = BLOCK + HALF
OUT_SPLIT = 4
VMEM_LIMIT_BYTES = 56 * 1024 * 1024

F32 = jnp.float32
BF16 = jnp.bfloat16


def _silu(v):
    return v / (1.0 + jnp.exp2(v * (-LOG2E)))


def _layer_kernel(layer, sinks_ref, x_ref, w_in32_ref, pw_ref, pscale_ref, w_out32_ref, tab_ref,
                  ln_g_ref, ln_b_ref, o_ref,
                  w_in_ref, w_out_ref, xb_scr, uext_scr, kext_scr, vext_scr, q_scr, yattn_scr, gate_scr, y_scr):
    s_idx = pl.program_id(1)
    T = TILE
    H = MAX_POOL_WINDOW

    @pl.when((pl.program_id(0) == 0) & (s_idx == 0))
    def _():
        for c0 in range(0, IN_WIDTH, N_CHUNK):
            w = w_in32_ref[:, c0:c0 + N_CHUNK]
            if COL_Q <= c0 < COL_K:
                w = w * (HEAD_DIM ** -0.5)
            w_in_ref[:, c0:c0 + N_CHUNK] = w.astype(BF16)
        for c0 in range(0, D_MODEL, N_CHUNK):
            w_out_ref[:, c0:c0 + N_CHUNK] = w_out32_ref[:, c0:c0 + N_CHUNK].astype(BF16)

    uext_scr[0:H, :] = uext_scr[T:T + H, :]
    kext_scr[:, 0:BLOCK, :] = kext_scr[:, T:T + BLOCK, :]
    vext_scr[:, 0:BLOCK] = vext_scr[:, T:T + BLOCK]

    @pl.when(s_idx == 0)
    def _():
        uext_scr[0:H, :] = jnp.zeros((H, POOL_WIDTH), F32)
        kext_scr[:, 0:BLOCK, :] = jnp.zeros((N_KV_HEADS, BLOCK, HEAD_DIM), BF16)
        vext_scr[:, 0:BLOCK] = jnp.zeros((KV_WIDTH, BLOCK), BF16)

    xb_scr[...] = x_ref[...].astype(BF16)
    o_ref[...] = DEEPNORM_ALPHA * x_ref[...]

    def proj(col, width):
        return jnp.dot(xb_scr[...], w_in_ref[:, col:col + width], preferred_element_type=F32)

    def pool_u(p):
        c0 = p * N_CHUNK
        uext_scr[H:H + T, c0:c0 + N_CHUNK] = proj(COL_U + c0, N_CHUNK)

    def pool_mix(p):
        t_top = s_idx * T + lax.broadcasted_iota(jnp.int32, (H, POOL_GROUP), 0)
        c0 = p * N_CHUNK
        pooled = []
        for gi in range(2 * p, 2 * p + 2):
            w = POOL_WINDOWS[gi]
            g0 = gi * POOL_GROUP
            ext = uext_scr[:, g0:g0 + POOL_GROUP]
            acc, span = ext, 1
            while span < w:
                acc = acc + pltpu.roll(acc, span, axis=0)
                span *= 2
            top = acc[H:2 * H] / jnp.minimum(t_top + 1, w).astype(F32)
            mean = jnp.concatenate([top, acc[2 * H:H + T] * (1.0 / w)], axis=0)
            pooled.append((mean - ext[H:H + T]).astype(BF16))
        pooled = jnp.concatenate(pooled, axis=1)
        pw_a, pw_b = pw_ref[2 * p].astype(BF16), pw_ref[2 * p + 1].astype(BF16)
        zeros = jnp.zeros_like(pw_a)
        pw = jnp.concatenate([jnp.concatenate([pw_a, zeros], axis=1),
                              jnp.concatenate([zeros, pw_b], axis=1)], axis=0)
        mixed = jnp.dot(pooled, pw, preferred_element_type=F32)
        gate = _silu(proj(COL_GP + c0, N_CHUNK))
        y_scr[:, c0:c0 + N_CHUNK] = (mixed * pscale_ref[layer:layer + 1, c0:c0 + N_CHUNK] * gate).astype(BF16)

    def q_proj(c):
        q = proj(COL_Q + c * N_CHUNK, N_CHUNK).astype(BF16)
        for g in range(GQA_GROUP):
            q_scr[c * GQA_GROUP + g] = q[:, g * HEAD_DIM:(g + 1) * HEAD_DIM]

    def kv_proj():
        kv = proj(COL_K, 2 * KV_WIDTH)
        for h in range(N_KV_HEADS):
            kext_scr[h, BLOCK:BLOCK + T, :] = kv[:, h * HEAD_DIM:(h + 1) * HEAD_DIM].astype(BF16)
        vext_scr[:, BLOCK:BLOCK + T] = kv[:, KV_WIDTH:2 * KV_WIDTH].T.astype(BF16)

    def scores(h, i, a):
        r0 = i * BLOCK + a * HALF
        qs = q_scr[h * GQA_GROUP:(h + 1) * GQA_GROUP, r0:r0 + HALF, :]
        qs = qs.reshape(GQA_GROUP * HALF, HEAD_DIM)
        kwin = kext_scr[h, r0:r0 + KEYS, :]
        sc = lax.dot_general(kwin, qs, (((1,), (1,)), ((), ())),
                             preferred_element_type=F32)
        if i == 0:
            first = (s_idx == 0).astype(jnp.int32)
            return sc + tab_ref[first * (1 + a), h]
        return sc + tab_ref[0, h]

    def softmax_values(h, i, a, sc):
        sink = jnp.concatenate(
            [jnp.full((1, HALF), sinks_ref[layer, h * GQA_GROUP + g], F32) for g in range(GQA_GROUP)],
            axis=1)
        r0 = i * BLOCK
        vwin = vext_scr[h * HEAD_DIM:(h + 1) * HEAD_DIM, r0:r0 + 2 * BLOCK]
        m = jnp.maximum(jnp.max(sc, axis=0, keepdims=True), sink)
        e = jnp.exp(sc - m)
        l = jnp.sum(e, axis=0, keepdims=True) + jnp.exp(sink - m)
        e = e.astype(BF16)
        pad = jnp.zeros((HALF, GQA_GROUP * HALF), BF16)
        e = jnp.concatenate([e, pad] if a == 0 else [pad, e], axis=0)
        o = jnp.dot(vwin, e, preferred_element_type=F32)
        o = o * (1.0 / l)
        for g in range(GQA_GROUP):
            row = (h * GQA_GROUP + g) * HEAD_DIM
            yattn_scr[row:row + HEAD_DIM, r0 + a * HALF:r0 + (a + 1) * HALF] = o[:, g * HALF:(g + 1) * HALF]

    def gate_proj(c):
        c0 = c * N_CHUNK
        gate_scr[:, c0:c0 + N_CHUNK] = _silu(proj(COL_GA + c0, N_CHUNK))

    def gate_attention(c):
        c0 = c * N_CHUNK
        y_scr[:, POOL_WIDTH + c0:POOL_WIDTH + c0 + N_CHUNK] = (
            yattn_scr[c0:c0 + N_CHUNK, :].T * gate_scr[:, c0:c0 + N_CHUNK]).astype(BF16)

    def finish():
        gamma = ln_g_ref[layer:layer + 1, :]
        beta = ln_b_ref[layer:layer + 1, :]
        half = T // OUT_SPLIT
        for hf in range(OUT_SPLIT):
            hrows = slice(hf * half, (hf + 1) * half)
            acc = None
            for c0 in range(0, D_MODEL, N_CHUNK):
                blk = o_ref[hrows, c0:c0 + N_CHUNK] + jnp.dot(
                    y_scr[hrows, :], w_out_ref[:, c0:c0 + N_CHUNK], preferred_element_type=F32)
                o_ref[hrows, c0:c0 + N_CHUNK] = blk
                part = blk[:, :N_CHUNK // 2] + blk[:, N_CHUNK // 2:]
                acc = part if acc is None else acc + part
            mu_g = jnp.sum(acc, axis=-1, keepdims=True) * (1.0 / D_MODEL)
            for r in range(half // LN_ROWS):
                rows = slice(hf * half + r * LN_ROWS, hf * half + (r + 1) * LN_ROWS)
                zc = o_ref[rows, :] - mu_g[r * LN_ROWS:(r + 1) * LN_ROWS]
                var = jnp.mean(zc * zc, axis=-1, keepdims=True)
                o_ref[rows, :] = zc * lax.rsqrt(var + LN_EPS) * gamma + beta

    fillers = [lambda: pool_u(0), lambda: pool_u(1), lambda: pool_mix(0), lambda: gate_proj(0),
               lambda: pool_mix(1), lambda: gate_proj(1), lambda: gate_attention(0), lambda: None]
    blocks = [(h, i, a) for h in range(N_KV_HEADS) for i in range(T // BLOCK) for a in range(2)]
    kv_proj()
    q_proj(0)
    q_proj(1)
    pending = [scores(*blocks[k]) for k in range(LOOKAHEAD)]
    for k, (h, i, a) in enumerate(blocks):
        softmax_values(h, i, a, pending.pop(0))
        if k + LOOKAHEAD < len(blocks):
            pending.append(scores(*blocks[k + LOOKAHEAD]))
        if k % 4 == 3:
            fillers[k // 4]()
    gate_attention(1)
    finish()


def _t5_bucket(dist):
    max_exact = N_BUCKETS // 2
    ratio = (np.log(np.maximum(dist, 1).astype(np.float32) / np.float32(max_exact))
             / np.float32(math.log(MAX_DISTANCE / max_exact)))
    large = max_exact + (ratio * np.float32(N_BUCKETS - max_exact)).astype(np.int32)
    large = np.minimum(large, N_BUCKETS - 1)
    return np.where(dist < max_exact, dist, large)


def _table_kernel(by_dist_ref, tab_ref):
    lanes = 4 * BLOCK
    key = lax.broadcasted_iota(jnp.int32, (KEYS, HALF), 0)
    for hg in range(N_Q_HEADS):
        h, g = divmod(hg, GQA_GROUP)
        vec = jnp.concatenate([by_dist_ref[hg:hg + 1, :],
                               jnp.full((1, lanes - WINDOW), NEG_INF, F32)], axis=1)
        rolled = pltpu.roll(jnp.broadcast_to(vec, (KEYS, lanes)), 0, 1, stride=1, stride_axis=0)
        visible = rolled[:, BLOCK:BLOCK + HALF]
        cols = slice(g * HALF, (g + 1) * HALF)
        tab_ref[0, h, :, cols] = visible
        tab_ref[1, h, :, cols] = jnp.where(key >= BLOCK, visible, NEG_INF)
        tab_ref[2, h, :, cols] = jnp.where(key >= BLOCK - HALF, visible, NEG_INF)


def _bias_tables(rel_bias):
    onehot = _t5_bucket(np.arange(WINDOW))[:, None] == np.arange(N_BUCKETS)
    by_dist = jnp.sum(jnp.where(onehot[None], rel_bias.astype(F32).T[:, None, :], 0.0),
                      axis=-1)
    return pl.pallas_call(
        _table_kernel,
        out_shape=jax.ShapeDtypeStruct((3, N_KV_HEADS, KEYS, GQA_GROUP * HALF), F32),
        name="bias_tables",
    )(by_dist)


def _layer(layer, x, params, tab):
    B, S, D = x.shape
    T = TILE
    const = lambda *shape: pl.BlockSpec(shape, lambda b, s: (0,) * len(shape))
    of_layer = lambda *shape, **kw: pl.BlockSpec(
        (None,) + shape, lambda b, s: (layer,) + (0,) * len(shape), **kw)
    resident = dict(pipeline_mode=pl.Buffered(1))
    return pl.pallas_call(
        functools.partial(_layer_kernel, layer),
        out_shape=jax.ShapeDtypeStruct((B, S, D), F32),
        grid=(B, S // T),
        in_specs=[
            pl.BlockSpec(memory_space=pltpu.SMEM),
            pl.BlockSpec((None, T, D), lambda b, s: (b, s, 0)),
            of_layer(D, IN_WIDTH, **resident),
            of_layer(len(POOL_WINDOWS), POOL_GROUP, POOL_GROUP),
            const(DEPTH, POOL_WIDTH),
            of_layer(D, D, **resident),
            const(3, N_KV_HEADS, KEYS, GQA_GROUP * HALF),
            const(DEPTH, D),
            const(DEPTH, D),
        ],
        out_specs=pl.BlockSpec((None, T, D), lambda b, s: (b, s, 0)),
        scratch_shapes=[
            pltpu.VMEM((D, IN_WIDTH), BF16),
            pltpu.VMEM((D, D), BF16),
            pltpu.VMEM((T, D), BF16),
            pltpu.VMEM((MAX_POOL_WINDOW + T, POOL_WIDTH), F32),
            pltpu.VMEM((N_KV_HEADS, BLOCK + T, HEAD_DIM), BF16),
            pltpu.VMEM((KV_WIDTH, BLOCK + T), BF16),
            pltpu.VMEM((N_Q_HEADS, T, HEAD_DIM), BF16),
            pltpu.VMEM((ATTN_WIDTH, T), F32),
            pltpu.VMEM((T, ATTN_WIDTH), F32),
            pltpu.VMEM((T, D), BF16),
        ],
        compiler_params=pltpu.CompilerParams(
            dimension_semantics=("arbitrary", "arbitrary"),
            vmem_limit_bytes=VMEM_LIMIT_BYTES),
        name="hybrid_layer",
    )(params[0], x, *params[1:5], tab, *params[5:])


@jax.jit
def kernel(x, ln_g, ln_b, w_in, pool_w, pool_scale, sinks, w_out, rel_bias):
    params = (sinks.astype(F32), w_in, pool_w, pool_scale, w_out, ln_g, ln_b)
    tab = _bias_tables(rel_bias)
    for layer in range(DEPTH):
        x = _layer(layer, x, params, tab)
    return x
```

```python
import functools
import math

import jax
import jax.numpy as jnp
from jax import lax
from jax.experimental import pallas as pl
from jax.experimental.pallas import tpu as pltpu
import numpy as np

D_MODEL = 1024
DEPTH = 2
POOL_WIDTH = 512
POOL_WINDOWS = (2, 4, 8, 16)
POOL_GROUP = 128
MAX_POOL_WINDOW = 16
HEAD_DIM = 64
N_Q_HEADS = 8
N_KV_HEADS = 2
GQA_GROUP = 4
ATTN_WIDTH = 512
KV_WIDTH = 128
WINDOW = 128
BLOCK = 128
N_BUCKETS = 32
MAX_DISTANCE = 128
IN_WIDTH = 2304
COL_U, COL_GP, COL_Q, COL_K, COL_V, COL_GA = 0, 512, 1024, 1536, 1664, 1792
DEEPNORM_ALPHA = (2.0 * DEPTH) ** 0.25
LN_EPS = 1e-5
NEG_INF = -1e30
LOG2E = math.log2(math.e)

TILE = 1024
N_CHUNK = 256
LN_ROWS = 64
LOOKAHEAD = 5
HALF = BLOCK // 2
KEYS = BLOCK + HALF
OUT_SPLIT = 4
VMEM_LIMIT_BYTES = 56 * 1024 * 1024

F32 = jnp.float32
BF16 = jnp.bfloat16


def _silu(v):
    return v / (1.0 + jnp.exp2(v * (-LOG2E)))


def _layer_kernel(layer, sinks_ref, x_ref, w_in32_ref, pw_ref, pscale_ref, w_out32_ref, tab_ref,
                  ln_g_ref, ln_b_ref, o_ref,
                  w_in_ref, w_out_ref, xb_scr, uext_scr, kext_scr, vext_scr, q_scr, yattn_scr, gate_scr, y_scr):
    s_idx = pl.program_id(1)
    T = TILE
    H = MAX_POOL_WINDOW

    @pl.when((pl.program_id(0) == 0) & (s_idx == 0))
    def _():
        for c0 in range(0, IN_WIDTH, N_CHUNK):
            w = w_in32_ref[:, c0:c0 + N_CHUNK]
            if COL_Q <= c0 < COL_K:
                w = w * (HEAD_DIM ** -0.5)
            w_in_ref[:, c0:c0 + N_CHUNK] = w.astype(BF16)
        for c0 in range(0, D_MODEL, N_CHUNK):
            w_out_ref[:, c0:c0 + N_CHUNK] = w_out32_ref[:, c0:c0 + N_CHUNK].astype(BF16)

    uext_scr[0:H, :] = uext_scr[T:T + H, :]
    kext_scr[:, 0:BLOCK, :] = kext_scr[:, T:T + BLOCK, :]
    vext_scr[:, 0:BLOCK] = vext_scr[:, T:T + BLOCK]

    @pl.when(s_idx == 0)
    def _():
        uext_scr[0:H, :] = jnp.zeros((H, POOL_WIDTH), F32)
        kext_scr[:, 0:BLOCK, :] = jnp.zeros((N_KV_HEADS, BLOCK, HEAD_DIM), BF16)
        vext_scr[:, 0:BLOCK] = jnp.zeros((KV_WIDTH, BLOCK), BF16)

    xb_scr[...] = x_ref[...].astype(BF16)
    o_ref[...] = DEEPNORM_ALPHA * x_ref[...]

    def proj(col, width):
        return jnp.dot(xb_scr[...], w_in_ref[:, col:col + width], preferred_element_type=F32)

    def pool_u(p):
        c0 = p * N_CHUNK
        uext_scr[H:H + T, c0:c0 + N_CHUNK] = proj(COL_U + c0, N_CHUNK)

    def pool_mix(p):
        t_top = s_idx * T + lax.broadcasted_iota(jnp.int32, (H, POOL_GROUP), 0)
        c0 = p * N_CHUNK
        pooled = []
        for gi in range(2 * p, 2 * p + 2):
            w = POOL_WINDOWS[gi]
            g0 = gi * POOL_GROUP
            ext = uext_scr[:, g0:g0 + POOL_GROUP]
            acc, span = ext, 1
            while span < w:
                acc = acc + pltpu.roll(acc, span, axis=0)
                span *= 2
            top = acc[H:2 * H] / jnp.minimum(t_top + 1, w).astype(F32)
            mean = jnp.concatenate([top, acc[2 * H:H + T] * (1.0 / w)], axis=0)
            pooled.append((mean - ext[H:H + T]).astype(BF16))
        pooled = jnp.concatenate(pooled, axis=1)
        pw_a, pw_b = pw_ref[2 * p].astype(BF16), pw_ref[2 * p + 1].astype(BF16)
        zeros = jnp.zeros_like(pw_a)
        pw = jnp.concatenate([jnp.concatenate([pw_a, zeros], axis=1),
                              jnp.concatenate([zeros, pw_b], axis=1)], axis=0)
        mixed = jnp.dot(pooled, pw, preferred_element_type=F32)
        gate = _silu(proj(COL_GP + c0, N_CHUNK))
        y_scr[:, c0:c0 + N_CHUNK] = (mixed * pscale_ref[layer:layer + 1, c0:c0 + N_CHUNK] * gate).astype(BF16)

    def q_proj(c):
        q = proj(COL_Q + c * N_CHUNK, N_CHUNK).astype(BF16)
        for g in range(GQA_GROUP):
            q_scr[c * GQA_GROUP + g] = q[:, g * HEAD_DIM:(g + 1) * HEAD_DIM]

    def kv_proj():
        kv = proj(COL_K, 2 * KV_WIDTH)
        for h in range(N_KV_HEADS):
            kext_scr[h, BLOCK:BLOCK + T, :] = kv[:, h * HEAD_DIM:(h + 1) * HEAD_DIM].astype(BF16)
        vext_scr[:, BLOCK:BLOCK + T] = kv[:, KV_WIDTH:2 * KV_WIDTH].T.astype(BF16)

    def scores(h, i, a):
        r0 = i * BLOCK + a * HALF
        qs = q_scr[h * GQA_GROUP:(h + 1) * GQA_GROUP, r0:r0 + HALF, :]
        qs = qs.reshape(GQA_GROUP * HALF, HEAD_DIM)
        kwin = kext_scr[h, r0:r0 + KEYS, :]
        sc = lax.dot_general(kwin, qs, (((1,), (1,)), ((), ())),
                             preferred_element_type=F32)
        if i == 0:
            first = (s_idx == 0).astype(jnp.int32)
            return sc + tab_ref[first * (1 + a), h]
        return sc + tab_ref[0, h]

    def softmax_values(h, i, a, sc):
        sink = jnp.concatenate(
            [jnp.full((1, HALF), sinks_ref[layer, h * GQA_GROUP + g], F32) for g in range(GQA_GROUP)],
            axis=1)
        r0 = i * BLOCK
        vwin = vext_scr[h * HEAD_DIM:(h + 1) * HEAD_DIM, r0:r0 + 2 * BLOCK]
        m = jnp.maximum(jnp.max(sc, axis=0, keepdims=True), sink)
        e = jnp.exp(sc - m)
        l = jnp.sum(e, axis=0, keepdims=True) + jnp.exp(sink - m)
        e = e.astype(BF16)
        pad = jnp.zeros((HALF, GQA_GROUP * HALF), BF16)
        e = jnp.concatenate([e, pad] if a == 0 else [pad, e], axis=0)
        o = jnp.dot(vwin, e, preferred_element_type=F32)
        o = o * (1.0 / l)
        for g in range(GQA_GROUP):
            row = (h * GQA_GROUP + g) * HEAD_DIM
            yattn_scr[row:row + HEAD_DIM, r0 + a * HALF:r0 + (a + 1) * HALF] = o[:, g * HALF:(g + 1) * HALF]

    def gate_proj(c):
        c0 = c * N_CHUNK
        gate_scr[:, c0:c0 + N_CHUNK] = _silu(proj(COL_GA + c0, N_CHUNK))

    def gate_attention(c):
        c0 = c * N_CHUNK
        y_scr[:, POOL_WIDTH + c0:POOL_WIDTH + c0 + N_CHUNK] = (
            yattn_scr[c0:c0 + N_CHUNK, :].T * gate_scr[:, c0:c0 + N_CHUNK]).astype(BF16)

    def zero_of(v):
        bits = lax.bitcast_convert_type(v, jnp.uint32)
        bits = lax.shift_right_logical(lax.shift_right_logical(bits, jnp.uint32(16)), jnp.uint32(16))
        return lax.bitcast_convert_type(bits, F32)

    def finish():
        gamma = ln_g_ref[layer:layer + 1, :]
        beta = ln_b_ref[layer:layer + 1, :]
        group = T // OUT_SPLIT

        def project(hf):
            hrows = slice(hf * group, (hf + 1) * group)
            acc, marks = None, []
            for c0 in range(0, D_MODEL, N_CHUNK):
                blk = o_ref[hrows, c0:c0 + N_CHUNK] + jnp.dot(
                    y_scr[hrows, :], w_out_ref[:, c0:c0 + N_CHUNK], preferred_element_type=F32)
                o_ref[hrows, c0:c0 + N_CHUNK] = blk
                part = blk[:, :N_CHUNK // 2] + blk[:, N_CHUNK // 2:]
                acc = part if acc is None else acc + part
                marks.append(zero_of(blk[0:8, 0:1]))
            return jnp.sum(acc, axis=-1, keepdims=True) * (1.0 / D_MODEL), marks

        def normalise(hf, mu_g, marks):
            for r in range(group // LN_ROWS):
                rows = slice(hf * group + r * LN_ROWS, hf * group + (r + 1) * LN_ROWS)
                mu = mu_g[r * LN_ROWS:(r + 1) * LN_ROWS]
                if marks is not None and r >= 1:
                    mu = (mu.reshape(LN_ROWS // 8, 8, 1) + marks[r - 1][None]).reshape(LN_ROWS, 1)
                zc = o_ref[rows, :] - mu
                var = jnp.mean(zc * zc, axis=-1, keepdims=True)
                o_ref[rows, :] = zc * lax.rsqrt(var + LN_EPS) * gamma + beta

        mu_prev, _ = project(0)
        for hf in range(1, OUT_SPLIT):
            mu_g, marks = project(hf)
            normalise(hf - 1, mu_prev, marks)
            mu_prev = mu_g
        normalise(OUT_SPLIT - 1, mu_prev, None)

    fillers = [lambda: pool_u(0), lambda: pool_u(1), lambda: pool_mix(0), lambda: gate_proj(0),
               lambda: pool_mix(1), lambda: gate_proj(1), lambda: gate_attention(0), lambda: None]
    blocks = [(h, i, a) for h in range(N_KV_HEADS) for i in range(T // BLOCK) for a in range(2)]
    kv_proj()
    q_proj(0)
    q_proj(1)
    pending = [scores(*blocks[k]) for k in range(LOOKAHEAD)]
    for k, (h, i, a) in enumerate(blocks):
        softmax_values(h, i, a, pending.pop(0))
        if k + LOOKAHEAD < len(blocks):
            pending.append(scores(*blocks[k + LOOKAHEAD]))
        if k % 4 == 3:
            fillers[k // 4]()
    gate_attention(1)
    finish()


def _t5_bucket(dist):
    max_exact = N_BUCKETS // 2
    ratio = (np.log(np.maximum(dist, 1).astype(np.float32) / np.float32(max_exact))
             / np.float32(math.log(MAX_DISTANCE / max_exact)))
    large = max_exact + (ratio * np.float32(N_BUCKETS - max_exact)).astype(np.int32)
    large = np.minimum(large, N_BUCKETS - 1)
    return np.where(dist < max_exact, dist, large)


def _table_kernel(by_dist_ref, tab_ref):
    lanes = 4 * BLOCK
    key = lax.broadcasted_iota(jnp.int32, (KEYS, HALF), 0)
    for hg in range(N_Q_HEADS):
        h, g = divmod(hg, GQA_GROUP)
        vec = jnp.concatenate([by_dist_ref[hg:hg + 1, :],
                               jnp.full((1, lanes - WINDOW), NEG_INF, F32)], axis=1)
        rolled = pltpu.roll(jnp.broadcast_to(vec, (KEYS, lanes)), 0, 1, stride=1, stride_axis=0)
        visible = rolled[:, BLOCK:BLOCK + HALF]
        cols = slice(g * HALF, (g + 1) * HALF)
        tab_ref[0, h, :, cols] = visible
        tab_ref[1, h, :, cols] = jnp.where(key >= BLOCK, visible, NEG_INF)
        tab_ref[2, h, :, cols] = jnp.where(key >= BLOCK - HALF, visible, NEG_INF)


def _bias_tables(rel_bias):
    onehot = _t5_bucket(np.arange(WINDOW))[:, None] == np.arange(N_BUCKETS)
    by_dist = jnp.sum(jnp.where(onehot[None], rel_bias.astype(F32).T[:, None, :], 0.0),
                      axis=-1)
    return pl.pallas_call(
        _table_kernel,
        out_shape=jax.ShapeDtypeStruct((3, N_KV_HEADS, KEYS, GQA_GROUP * HALF), F32),
        name="bias_tables",
    )(by_dist)


def _layer(layer, x, params, tab):
    B, S, D = x.shape
    T = TILE
    const = lambda *shape: pl.BlockSpec(shape, lambda b, s: (0,) * len(shape))
    of_layer = lambda *shape, **kw: pl.BlockSpec(
        (None,) + shape, lambda b, s: (layer,) + (0,) * len(shape), **kw)
    resident = dict(pipeline_mode=pl.Buffered(1))
    return pl.pallas_call(
        functools.partial(_layer_kernel, layer),
        out_shape=jax.ShapeDtypeStruct((B, S, D), F32),
        grid=(B, S // T),
        in_specs=[
            pl.BlockSpec(memory_space=pltpu.SMEM),
            pl.BlockSpec((None, T, D), lambda b, s: (b, s, 0)),
            of_layer(D, IN_WIDTH, **resident),
            of_layer(len(POOL_WINDOWS), POOL_GROUP, POOL_GROUP),
            const(DEPTH, POOL_WIDTH),
            of_layer(D, D, **resident),
            const(3, N_KV_HEADS, KEYS, GQA_GROUP * HALF),
            const(DEPTH, D),
            const(DEPTH, D),
        ],
        out_specs=pl.BlockSpec((None, T, D), lambda b, s: (b, s, 0)),
        scratch_shapes=[
            pltpu.VMEM((D, IN_WIDTH), BF16),
            pltpu.VMEM((D, D), BF16),
            pltpu.VMEM((T, D), BF16),
            pltpu.VMEM((MAX_POOL_WINDOW + T, POOL_WIDTH), F32),
            pltpu.VMEM((N_KV_HEADS, BLOCK + T, HEAD_DIM), BF16),
            pltpu.VMEM((KV_WIDTH, BLOCK + T), BF16),
            pltpu.VMEM((N_Q_HEADS, T, HEAD_DIM), BF16),
            pltpu.VMEM((ATTN_WIDTH, T), F32),
            pltpu.VMEM((T, ATTN_WIDTH), F32),
            pltpu.VMEM((T, D), BF16),
        ],
        compiler_params=pltpu.CompilerParams(
            dimension_semantics=("arbitrary", "arbitrary"),
            vmem_limit_bytes=VMEM_LIMIT_BYTES),
        name="hybrid_layer",
    )(params[0], x, *params[1:5], tab, *params[5:])


@jax.jit
def kernel(x, ln_g, ln_b, w_in, pool_w, pool_scale, sinks, w_out, rel_bias):
    params = (sinks.astype(F32), w_in, pool_w, pool_scale, w_out, ln_g, ln_b)
    tab = _bias_tables(rel_bias)
    for layer in range(DEPTH):
        x = _layer(layer, x, params, tab)
    return x
```

```python
import functools
import math

import jax
import jax.numpy as jnp
from jax import lax
from jax.experimental import pallas as pl
from jax.experimental.pallas import tpu as pltpu
import numpy as np

D_MODEL = 1024
DEPTH = 2
POOL_WIDTH = 512
POOL_WINDOWS = (2, 4, 8, 16)
POOL_GROUP = 128
MAX_POOL_WINDOW = 16
HEAD_DIM = 64
N_Q_HEADS = 8
N_KV_HEADS = 2
GQA_GROUP = 4
ATTN_WIDTH = 512
KV_WIDTH = 128
WINDOW = 128
BLOCK = 128
N_BUCKETS = 32
MAX_DISTANCE = 128
IN_WIDTH = 2304
COL_U, COL_GP, COL_Q, COL_K, COL_V, COL_GA = 0, 512, 1024, 1536, 1664, 1792
DEEPNORM_ALPHA = (2.0 * DEPTH) ** 0.25
LN_EPS = 1e-5
NEG_INF = -1e30
LOG2E = math.log2(math.e)

TILE = 1024
N_CHUNK = 256
LN_ROWS = 32
LOOKAHEAD = 5
HALF = BLOCK // 2
KEYS = BLOCK + HALF
OUT_SPLIT = 4
VMEM_LIMIT_BYTES = 56 * 1024 * 1024

F32 = jnp.float32
BF16 = jnp.bfloat16


def _silu(v):
    return v / (1.0 + jnp.exp2(v * (-LOG2E)))


def _layer_kernel(layer, sinks_ref, x_ref, w_in32_ref, pw_ref, pscale_ref, w_out32_ref, tab_ref,
                  ln_g_ref, ln_b_ref, o_ref,
                  w_in_ref, w_out_ref, xb_scr, uext_scr, kext_scr, vext_scr, q_scr, yattn_scr, gate_scr, y_scr):
    s_idx = pl.program_id(1)
    T = TILE
    H = MAX_POOL_WINDOW

    @pl.when((pl.program_id(0) == 0) & (s_idx == 0))
    def _():
        for c0 in range(0, IN_WIDTH, N_CHUNK):
            w = w_in32_ref[:, c0:c0 + N_CHUNK]
            if COL_Q <= c0 < COL_K:
                w = w * (HEAD_DIM ** -0.5)
            w_in_ref[:, c0:c0 + N_CHUNK] = w.astype(BF16)
        for c0 in range(0, D_MODEL, N_CHUNK):
            w_out_ref[:, c0:c0 + N_CHUNK] = w_out32_ref[:, c0:c0 + N_CHUNK].astype(BF16)

    uext_scr[0:H, :] = uext_scr[T:T + H, :]
    kext_scr[:, 0:BLOCK, :] = kext_scr[:, T:T + BLOCK, :]
    vext_scr[:, 0:BLOCK] = vext_scr[:, T:T + BLOCK]

    @pl.when(s_idx == 0)
    def _():
        uext_scr[0:H, :] = jnp.zeros((H, POOL_WIDTH), F32)
        kext_scr[:, 0:BLOCK, :] = jnp.zeros((N_KV_HEADS, BLOCK, HEAD_DIM), BF16)
        vext_scr[:, 0:BLOCK] = jnp.zeros((KV_WIDTH, BLOCK), BF16)

    xb_scr[...] = x_ref[...].astype(BF16)
    o_ref[...] = DEEPNORM_ALPHA * x_ref[...]

    def proj(col, width):
        return jnp.dot(xb_scr[...], w_in_ref[:, col:col + width], preferred_element_type=F32)

    def pool_u(p):
        c0 = p * N_CHUNK
        uext_scr[H:H + T, c0:c0 + N_CHUNK] = proj(COL_U + c0, N_CHUNK)

    def pool_mix(p):
        t_top = s_idx * T + lax.broadcasted_iota(jnp.int32, (H, POOL_GROUP), 0)
        c0 = p * N_CHUNK
        pooled = []
        for gi in range(2 * p, 2 * p + 2):
            w = POOL_WINDOWS[gi]
            g0 = gi * POOL_GROUP
            ext = uext_scr[:, g0:g0 + POOL_GROUP]
            acc, span = ext, 1
            while span < w:
                acc = acc + pltpu.roll(acc, span, axis=0)
                span *= 2
            top = acc[H:2 * H] / jnp.minimum(t_top + 1, w).astype(F32)
            mean = jnp.concatenate([top, acc[2 * H:H + T] * (1.0 / w)], axis=0)
            pooled.append((mean - ext[H:H + T]).astype(BF16))
        pooled = jnp.concatenate(pooled, axis=1)
        pw_a, pw_b = pw_ref[2 * p].astype(BF16), pw_ref[2 * p + 1].astype(BF16)
        zeros = jnp.zeros_like(pw_a)
        pw = jnp.concatenate([jnp.concatenate([pw_a, zeros], axis=1),
                              jnp.concatenate([zeros, pw_b], axis=1)], axis=0)
        mixed = jnp.dot(pooled, pw, preferred_element_type=F32)
        gate = _silu(proj(COL_GP + c0, N_CHUNK))
        y_scr[:, c0:c0 + N_CHUNK] = (mixed * pscale_ref[layer:layer + 1, c0:c0 + N_CHUNK] * gate).astype(BF16)

    def q_proj(c):
        q = proj(COL_Q + c * N_CHUNK, N_CHUNK).astype(BF16)
        for g in range(GQA_GROUP):
            q_scr[c * GQA_GROUP + g] = q[:, g * HEAD_DIM:(g + 1) * HEAD_DIM]

    def kv_proj():
        kv = proj(COL_K, 2 * KV_WIDTH)
        for h in range(N_KV_HEADS):
            kext_scr[h, BLOCK:BLOCK + T, :] = kv[:, h * HEAD_DIM:(h + 1) * HEAD_DIM].astype(BF16)
        vext_scr[:, BLOCK:BLOCK + T] = kv[:, KV_WIDTH:2 * KV_WIDTH].T.astype(BF16)

    def scores(h, i, a):
        r0 = i * BLOCK + a * HALF
        qs = q_scr[h * GQA_GROUP:(h + 1) * GQA_GROUP, r0:r0 + HALF, :]
        qs = qs.reshape(GQA_GROUP * HALF, HEAD_DIM)
        kwin = kext_scr[h, r0:r0 + KEYS, :]
        sc = lax.dot_general(kwin, qs, (((1,), (1,)), ((), ())),
                             preferred_element_type=F32)
        if i == 0:
            first = (s_idx == 0).astype(jnp.int32)
            return sc + tab_ref[first * (1 + a), h]
        return sc + tab_ref[0, h]

    def softmax_values(h, i, a, sc):
        sink = jnp.concatenate(
            [jnp.full((1, HALF), sinks_ref[layer, h * GQA_GROUP + g], F32) for g in range(GQA_GROUP)],
            axis=1)
        r0 = i * BLOCK
        vwin = vext_scr[h * HEAD_DIM:(h + 1) * HEAD_DIM, r0:r0 + 2 * BLOCK]
        m = jnp.maximum(jnp.max(sc, axis=0, keepdims=True), sink)
        e = jnp.exp(sc - m)
        l = jnp.sum(e, axis=0, keepdims=True) + jnp.exp(sink - m)
        e = e.astype(BF16)
        pad = jnp.zeros((HALF, GQA_GROUP * HALF), BF16)
        e = jnp.concatenate([e, pad] if a == 0 else [pad, e], axis=0)
        o = jnp.dot(vwin, e, preferred_element_type=F32)
        o = o * (1.0 / l)
        for g in range(GQA_GROUP):
            row = (h * GQA_GROUP + g) * HEAD_DIM
            yattn_scr[row:row + HEAD_DIM, r0 + a * HALF:r0 + (a + 1) * HALF] = o[:, g * HALF:(g + 1) * HALF]

    def gate_proj(c):
        c0 = c * N_CHUNK
        gate_scr[:, c0:c0 + N_CHUNK] = _silu(proj(COL_GA + c0, N_CHUNK))

    def gate_attention(c):
        c0 = c * N_CHUNK
        y_scr[:, POOL_WIDTH + c0:POOL_WIDTH + c0 + N_CHUNK] = (
            yattn_scr[c0:c0 + N_CHUNK, :].T * gate_scr[:, c0:c0 + N_CHUNK]).astype(BF16)

    def zero_of(v):
        bits = lax.bitcast_convert_type(v, jnp.uint32)
        bits = lax.shift_right_logical(lax.shift_right_logical(bits, jnp.uint32(16)), jnp.uint32(16))
        return lax.bitcast_convert_type(bits, F32)

    def finish():
        gamma = ln_g_ref[layer:layer + 1, :]
        beta = ln_b_ref[layer:layer + 1, :]
        group = T // OUT_SPLIT

        def project(hf):
            hrows = slice(hf * group, (hf + 1) * group)
            acc, marks = None, []
            for c0 in range(0, D_MODEL, N_CHUNK):
                blk = o_ref[hrows, c0:c0 + N_CHUNK] + jnp.dot(
                    y_scr[hrows, :], w_out_ref[:, c0:c0 + N_CHUNK], preferred_element_type=F32)
                o_ref[hrows, c0:c0 + N_CHUNK] = blk
                part = blk[:, :N_CHUNK // 2] + blk[:, N_CHUNK // 2:]
                acc = part if acc is None else acc + part
                marks.append(zero_of(blk[0:8, 0:1]))
            return jnp.sum(acc, axis=-1, keepdims=True) * (1.0 / D_MODEL), marks

        def normalise(hf, mu_g, marks):
            for r in range(group // LN_ROWS):
                rows = slice(hf * group + r * LN_ROWS, hf * group + (r + 1) * LN_ROWS)
                mu = mu_g[r * LN_ROWS:(r + 1) * LN_ROWS]
                if marks is not None and r >= 1:
                    mark = marks[(r - 1) * len(marks) * LN_ROWS // group]
                    mu = (mu.reshape(LN_ROWS // 8, 8, 1) + mark[None]).reshape(LN_ROWS, 1)
                zc = o_ref[rows, :] - mu
                var = jnp.mean(zc * zc, axis=-1, keepdims=True)
                o_ref[rows, :] = zc * lax.rsqrt(var + LN_EPS) * gamma + beta

        mu_prev, _ = project(0)
        for hf in range(1, OUT_SPLIT):
            mu_g, marks = project(hf)
            normalise(hf - 1, mu_prev, marks)
            mu_prev = mu_g
        normalise(OUT_SPLIT - 1, mu_prev, None)

    fillers = [lambda: pool_u(0), lambda: pool_u(1), lambda: pool_mix(0), lambda: gate_proj(0),
               lambda: pool_mix(1), lambda: gate_proj(1), lambda: gate_attention(0), lambda: None]
    blocks = [(h, i, a) for h in range(N_KV_HEADS) for i in range(T // BLOCK) for a in range(2)]
    kv_proj()
    q_proj(0)
    q_proj(1)
    pending = [scores(*blocks[k]) for k in range(LOOKAHEAD)]
    for k, (h, i, a) in enumerate(blocks):
        softmax_values(h, i, a, pending.pop(0))
        if k + LOOKAHEAD < len(blocks):
            pending.append(scores(*blocks[k + LOOKAHEAD]))
        if k % 4 == 3:
            fillers[k // 4]()
    gate_attention(1)
    finish()


def _t5_bucket(dist):
    max_exact = N_BUCKETS // 2
    ratio = (np.log(np.maximum(dist, 1).astype(np.float32) / np.float32(max_exact))
             / np.float32(math.log(MAX_DISTANCE / max_exact)))
    large = max_exact + (ratio * np.float32(N_BUCKETS - max_exact)).astype(np.int32)
    large = np.minimum(large, N_BUCKETS - 1)
    return np.where(dist < max_exact, dist, large)


def _table_kernel(by_dist_ref, tab_ref):
    lanes = 4 * BLOCK
    key = lax.broadcasted_iota(jnp.int32, (KEYS, HALF), 0)
    for hg in range(N_Q_HEADS):
        h, g = divmod(hg, GQA_GROUP)
        vec = jnp.concatenate([by_dist_ref[hg:hg + 1, :],
                               jnp.full((1, lanes - WINDOW), NEG_INF, F32)], axis=1)
        rolled = pltpu.roll(jnp.broadcast_to(vec, (KEYS, lanes)), 0, 1, stride=1, stride_axis=0)
        visible = rolled[:, BLOCK:BLOCK + HALF]
        cols = slice(g * HALF, (g + 1) * HALF)
        tab_ref[0, h, :, cols] = visible
        tab_ref[1, h, :, cols] = jnp.where(key >= BLOCK, visible, NEG_INF)
        tab_ref[2, h, :, cols] = jnp.where(key >= BLOCK - HALF, visible, NEG_INF)


def _bias_tables(rel_bias):
    onehot = _t5_bucket(np.arange(WINDOW))[:, None] == np.arange(N_BUCKETS)
    by_dist = jnp.sum(jnp.where(onehot[None], rel_bias.astype(F32).T[:, None, :], 0.0),
                      axis=-1)
    return pl.pallas_call(
        _table_kernel,
        out_shape=jax.ShapeDtypeStruct((3, N_KV_HEADS, KEYS, GQA_GROUP * HALF), F32),
        name="bias_tables",
    )(by_dist)


def _layer(layer, x, params, tab):
    B, S, D = x.shape
    T = TILE
    const = lambda *shape: pl.BlockSpec(shape, lambda b, s: (0,) * len(shape))
    of_layer = lambda *shape, **kw: pl.BlockSpec(
        (None,) + shape, lambda b, s: (layer,) + (0,) * len(shape), **kw)
    resident = dict(pipeline_mode=pl.Buffered(1))
    return pl.pallas_call(
        functools.partial(_layer_kernel, layer),
        out_shape=jax.ShapeDtypeStruct((B, S, D), F32),
        grid=(B, S // T),
        in_specs=[
            pl.BlockSpec(memory_space=pltpu.SMEM),
            pl.BlockSpec((None, T, D), lambda b, s: (b, s, 0)),
            of_layer(D, IN_WIDTH, **resident),
            of_layer(len(POOL_WINDOWS), POOL_GROUP, POOL_GROUP),
            const(DEPTH, POOL_WIDTH),
            of_layer(D, D, **resident),
            const(3, N_KV_HEADS, KEYS, GQA_GROUP * HALF),
            const(DEPTH, D),
            const(DEPTH, D),
        ],
        out_specs=pl.BlockSpec((None, T, D), lambda b, s: (b, s, 0)),
        scratch_shapes=[
            pltpu.VMEM((D, IN_WIDTH), BF16),
            pltpu.VMEM((D, D), BF16),
            pltpu.VMEM((T, D), BF16),
            pltpu.VMEM((MAX_POOL_WINDOW + T, POOL_WIDTH), F32),
            pltpu.VMEM((N_KV_HEADS, BLOCK + T, HEAD_DIM), BF16),
            pltpu.VMEM((KV_WIDTH, BLOCK + T), BF16),
            pltpu.VMEM((N_Q_HEADS, T, HEAD_DIM), BF16),
            pltpu.VMEM((ATTN_WIDTH, T), F32),
            pltpu.VMEM((T, ATTN_WIDTH), F32),
            pltpu.VMEM((T, D), BF16),
        ],
        compiler_params=pltpu.CompilerParams(
            dimension_semantics=("arbitrary", "arbitrary"),
            vmem_limit_bytes=VMEM_LIMIT_BYTES),
        name="hybrid_layer",
    )(params[0], x, *params[1:5], tab, *params[5:])


@jax.jit
def kernel(x, ln_g, ln_b, w_in, pool_w, pool_scale, sinks, w_out, rel_bias):
    params = (sinks.astype(F32), w_in, pool_w, pool_scale, w_out, ln_g, ln_b)
    tab = _bias_tables(rel_bias)
    for layer in range(DEPTH):
        x = _layer(layer, x, params, tab)
    return x
```

```python
import functools
import math

import jax
import jax.numpy as jnp
from jax import lax
from jax.experimental import pallas as pl
from jax.experimental.pallas import tpu as pltpu
import numpy as np

D_MODEL = 1024
DEPTH = 2
POOL_WIDTH = 512
POOL_WINDOWS = (2, 4, 8, 16)
POOL_GROUP = 128
MAX_POOL_WINDOW = 16
HEAD_DIM = 64
N_Q_HEADS = 8
N_KV_HEADS = 2
GQA_GROUP = 4
ATTN_WIDTH = 512
KV_WIDTH = 128
WINDOW = 128
BLOCK = 128
N_BUCKETS = 32
MAX_DISTANCE = 128
IN_WIDTH = 2304
COL_U, COL_GP, COL_Q, COL_K, COL_V, COL_GA = 0, 512, 1024, 1536, 1664, 1792
DEEPNORM_ALPHA = (2.0 * DEPTH) ** 0.25
LN_EPS = 1e-5
NEG_INF = -1e30
LOG2E = math.log2(math.e)

TILE = 1024
N_CHUNK = 256
LN_ROWS = 64
LOOKAHEAD = 5
HALF = BLOCK // 2
KEYS = BLOCK + HALF
OUT_SPLIT = 4
VMEM_LIMIT_BYTES = 56 * 1024 * 1024

F32 = jnp.float32
BF16 = jnp.bfloat16


def _silu(v):
    return v / (1.0 + jnp.exp2(v * (-LOG2E)))


def _layer_kernel(layer, sinks_ref, x_ref, w_in32_ref, pw_ref, pscale_ref, w_out32_ref, tab_ref,
                  ln_g_ref, ln_b_ref, o_ref,
                  w_in_ref, w_out_ref, xb_scr, uext_scr, kext_scr, vext_scr, q_scr, yattn_scr, gate_scr, y_scr):
    s_idx = pl.program_id(1)
    T = TILE
    H = MAX_POOL_WINDOW

    @pl.when((pl.program_id(0) == 0) & (s_idx == 0))
    def _():
        for c0 in range(0, IN_WIDTH, N_CHUNK):
            w = w_in32_ref[:, c0:c0 + N_CHUNK]
            if COL_Q <= c0 < COL_K:
                w = w * (HEAD_DIM ** -0.5)
            w_in_ref[:, c0:c0 + N_CHUNK] = w.astype(BF16)
        for c0 in range(0, D_MODEL, N_CHUNK):
            w_out_ref[:, c0:c0 + N_CHUNK] = w_out32_ref[:, c0:c0 + N_CHUNK].astype(BF16)

    uext_scr[0:H, :] = uext_scr[T:T + H, :]
    kext_scr[:, 0:BLOCK, :] = kext_scr[:, T:T + BLOCK, :]
    vext_scr[:, 0:BLOCK] = vext_scr[:, T:T + BLOCK]

    @pl.when(s_idx == 0)
    def _():
        uext_scr[0:H, :] = jnp.zeros((H, POOL_WIDTH), F32)
        kext_scr[:, 0:BLOCK, :] = jnp.zeros((N_KV_HEADS, BLOCK, HEAD_DIM), BF16)
        vext_scr[:, 0:BLOCK] = jnp.zeros((KV_WIDTH, BLOCK), BF16)

    xb_scr[...] = x_ref[...].astype(BF16)
    o_ref[...] = DEEPNORM_ALPHA * x_ref[...]

    def proj(col, width):
        return jnp.dot(xb_scr[...], w_in_ref[:, col:col + width], preferred_element_type=F32)

    def pool_u(p):
        c0 = p * N_CHUNK
        uext_scr[H:H + T, c0:c0 + N_CHUNK] = proj(COL_U + c0, N_CHUNK)

    def pool_mix(p):
        t_top = s_idx * T + lax.broadcasted_iota(jnp.int32, (H, POOL_GROUP), 0)
        c0 = p * N_CHUNK
        pooled = []
        for gi in range(2 * p, 2 * p + 2):
            w = POOL_WINDOWS[gi]
            g0 = gi * POOL_GROUP
            ext = uext_scr[:, g0:g0 + POOL_GROUP]
            acc, span = ext, 1
            while span < w:
                acc = acc + pltpu.roll(acc, span, axis=0)
                span *= 2
            top = acc[H:2 * H] / jnp.minimum(t_top + 1, w).astype(F32)
            mean = jnp.concatenate([top, acc[2 * H:H + T] * (1.0 / w)], axis=0)
            pooled.append((mean - ext[H:H + T]).astype(BF16))
        pooled = jnp.concatenate(pooled, axis=1)
        pw_a, pw_b = pw_ref[2 * p].astype(BF16), pw_ref[2 * p + 1].astype(BF16)
        zeros = jnp.zeros_like(pw_a)
        pw = jnp.concatenate([jnp.concatenate([pw_a, zeros], axis=1),
                              jnp.concatenate([zeros, pw_b], axis=1)], axis=0)
        mixed = jnp.dot(pooled, pw, preferred_element_type=F32)
        gate = _silu(proj(COL_GP + c0, N_CHUNK))
        y_scr[:, c0:c0 + N_CHUNK] = (mixed * pscale_ref[layer:layer + 1, c0:c0 + N_CHUNK] * gate).astype(BF16)

    def q_proj(c):
        q = proj(COL_Q + c * N_CHUNK, N_CHUNK).astype(BF16)
        for g in range(GQA_GROUP):
            q_scr[c * GQA_GROUP + g] = q[:, g * HEAD_DIM:(g + 1) * HEAD_DIM]

    def kv_proj():
        kv = proj(COL_K, 2 * KV_WIDTH)
        for h in range(N_KV_HEADS):
            kext_scr[h, BLOCK:BLOCK + T, :] = kv[:, h * HEAD_DIM:(h + 1) * HEAD_DIM].astype(BF16)
        vext_scr[:, BLOCK:BLOCK + T] = kv[:, KV_WIDTH:2 * KV_WIDTH].T.astype(BF16)

    def scores(h, i, a):
        r0 = i * BLOCK + a * HALF
        qs = q_scr[h * GQA_GROUP:(h + 1) * GQA_GROUP, r0:r0 + HALF, :]
        qs = qs.reshape(GQA_GROUP * HALF, HEAD_DIM)
        kwin = kext_scr[h, r0:r0 + KEYS, :]
        sc = lax.dot_general(kwin, qs, (((1,), (1,)), ((), ())),
                             preferred_element_type=F32)
        if i == 0:
            first = (s_idx == 0).astype(jnp.int32)
            return sc + tab_ref[first * (1 + a), h]
        return sc + tab_ref[0, h]

    def softmax_values(h, i, a, sc):
        sink = jnp.concatenate(
            [jnp.full((1, HALF), sinks_ref[layer, h * GQA_GROUP + g], F32) for g in range(GQA_GROUP)],
            axis=1)
        r0 = i * BLOCK
        vwin = vext_scr[h * HEAD_DIM:(h + 1) * HEAD_DIM, r0:r0 + 2 * BLOCK]
        m = jnp.maximum(jnp.max(sc, axis=0, keepdims=True), sink)
        e = jnp.exp(sc - m)
        l = jnp.sum(e, axis=0, keepdims=True) + jnp.exp(sink - m)
        e = e.astype(BF16)
        pad = jnp.zeros((HALF, GQA_GROUP * HALF), BF16)
        e = jnp.concatenate([e, pad] if a == 0 else [pad, e], axis=0)
        o = jnp.dot(vwin, e, preferred_element_type=F32)
        o = o * (1.0 / l)
        for g in range(GQA_GROUP):
            row = (h * GQA_GROUP + g) * HEAD_DIM
            yattn_scr[row:row + HEAD_DIM, r0 + a * HALF:r0 + (a + 1) * HALF] = o[:, g * HALF:(g + 1) * HALF]

    def gate_proj(c):
        c0 = c * N_CHUNK
        gate_scr[:, c0:c0 + N_CHUNK] = _silu(proj(COL_GA + c0, N_CHUNK))

    def gate_attention(c):
        c0 = c * N_CHUNK
        y_scr[:, POOL_WIDTH + c0:POOL_WIDTH + c0 + N_CHUNK] = (
            yattn_scr[c0:c0 + N_CHUNK, :].T * gate_scr[:, c0:c0 + N_CHUNK]).astype(BF16)

    def zero_of(v):
        bits = lax.bitcast_convert_type(v, jnp.uint32)
        bits = lax.shift_right_logical(lax.shift_right_logical(bits, jnp.uint32(16)), jnp.uint32(16))
        return lax.bitcast_convert_type(bits, F32)

    def finish():
        gamma = ln_g_ref[layer:layer + 1, :]
        beta = ln_b_ref[layer:layer + 1, :]
        group = T // OUT_SPLIT

        def project(hf):
            hrows = slice(hf * group, (hf + 1) * group)
            acc, marks = None, []
            for c0 in range(0, D_MODEL, N_CHUNK):
                blk = o_ref[hrows, c0:c0 + N_CHUNK] + jnp.dot(
                    y_scr[hrows, :], w_out_ref[:, c0:c0 + N_CHUNK], preferred_element_type=F32)
                o_ref[hrows, c0:c0 + N_CHUNK] = blk
                part = blk[:, :N_CHUNK // 2] + blk[:, N_CHUNK // 2:]
                acc = part if acc is None else acc + part
                marks.append(zero_of(blk[0:8, 0:1]))
            return jnp.sum(acc, axis=-1, keepdims=True) * (1.0 / D_MODEL), marks

        def normalise(hf, mu_g, marks):
            for r in range(group // LN_ROWS):
                rows = slice(hf * group + r * LN_ROWS, hf * group + (r + 1) * LN_ROWS)
                mu = mu_g[r * LN_ROWS:(r + 1) * LN_ROWS]
                if marks is not None and r >= 1:
                    mark = marks[(r - 1) // 2 if hf == OUT_SPLIT - 2 else r - 1]
                    mu = (mu.reshape(LN_ROWS // 8, 8, 1) + mark[None]).reshape(LN_ROWS, 1)
                zc = o_ref[rows, :] - mu
                var = jnp.mean(zc * zc, axis=-1, keepdims=True)
                o_ref[rows, :] = zc * lax.rsqrt(var + LN_EPS) * gamma + beta

        mu_prev, _ = project(0)
        for hf in range(1, OUT_SPLIT):
            mu_g, marks = project(hf)
            normalise(hf - 1, mu_prev, marks)
            mu_prev = mu_g
        normalise(OUT_SPLIT - 1, mu_prev, None)

    fillers = [lambda: pool_u(0), lambda: pool_u(1), lambda: pool_mix(0), lambda: gate_proj(0),
               lambda: pool_mix(1), lambda: gate_proj(1), lambda: gate_attention(0), lambda: None]
    blocks = [(h, i, a) for h in range(N_KV_HEADS) for i in range(T // BLOCK) for a in range(2)]
    kv_proj()
    q_proj(0)
    q_proj(1)
    pending = [scores(*blocks[k]) for k in range(LOOKAHEAD)]
    for k, (h, i, a) in enumerate(blocks):
        softmax_values(h, i, a, pending.pop(0))
        if k + LOOKAHEAD < len(blocks):
            pending.append(scores(*blocks[k + LOOKAHEAD]))
        if k % 4 == 3:
            fillers[k // 4]()
    gate_attention(1)
    finish()


def _t5_bucket(dist):
    max_exact = N_BUCKETS // 2
    ratio = (np.log(np.maximum(dist, 1).astype(np.float32) / np.float32(max_exact))
             / np.float32(math.log(MAX_DISTANCE / max_exact)))
    large = max_exact + (ratio * np.float32(N_BUCKETS - max_exact)).astype(np.int32)
    large = np.minimum(large, N_BUCKETS - 1)
    return np.where(dist < max_exact, dist, large)


def _table_kernel(by_dist_ref, tab_ref):
    lanes = 4 * BLOCK
    key = lax.broadcasted_iota(jnp.int32, (KEYS, HALF), 0)
    for hg in range(N_Q_HEADS):
        h, g = divmod(hg, GQA_GROUP)
        vec = jnp.concatenate([by_dist_ref[hg:hg + 1, :],
                               jnp.full((1, lanes - WINDOW), NEG_INF, F32)], axis=1)
        rolled = pltpu.roll(jnp.broadcast_to(vec, (KEYS, lanes)), 0, 1, stride=1, stride_axis=0)
        visible = rolled[:, BLOCK:BLOCK + HALF]
        cols = slice(g * HALF, (g + 1) * HALF)
        tab_ref[0, h, :, cols] = visible
        tab_ref[1, h, :, cols] = jnp.where(key >= BLOCK, visible, NEG_INF)
        tab_ref[2, h, :, cols] = jnp.where(key >= BLOCK - HALF, visible, NEG_INF)


def _bias_tables(rel_bias):
    onehot = _t5_bucket(np.arange(WINDOW))[:, None] == np.arange(N_BUCKETS)
    by_dist = jnp.sum(jnp.where(onehot[None], rel_bias.astype(F32).T[:, None, :], 0.0),
                      axis=-1)
    return pl.pallas_call(
        _table_kernel,
        out_shape=jax.ShapeDtypeStruct((3, N_KV_HEADS, KEYS, GQA_GROUP * HALF), F32),
        name="bias_tables",
    )(by_dist)


def _layer(layer, x, params, tab):
    B, S, D = x.shape
    T = TILE
    const = lambda *shape: pl.BlockSpec(shape, lambda b, s: (0,) * len(shape))
    of_layer = lambda *shape, **kw: pl.BlockSpec(
        (None,) + shape, lambda b, s: (layer,) + (0,) * len(shape), **kw)
    resident = dict(pipeline_mode=pl.Buffered(1))
    return pl.pallas_call(
        functools.partial(_layer_kernel, layer),
        out_shape=jax.ShapeDtypeStruct((B, S, D), F32),
        grid=(B, S // T),
        in_specs=[
            pl.BlockSpec(memory_space=pltpu.SMEM),
            pl.BlockSpec((None, T, D), lambda b, s: (b, s, 0)),
            of_layer(D, IN_WIDTH, **resident),
            of_layer(len(POOL_WINDOWS), POOL_GROUP, POOL_GROUP),
            const(DEPTH, POOL_WIDTH),
            of_layer(D, D, **resident),
            const(3, N_KV_HEADS, KEYS, GQA_GROUP * HALF),
            const(DEPTH, D),
            const(DEPTH, D),
        ],
        out_specs=pl.BlockSpec((None, T, D), lambda b, s: (b, s, 0)),
        scratch_shapes=[
            pltpu.VMEM((D, IN_WIDTH), BF16),
            pltpu.VMEM((D, D), BF16),
            pltpu.VMEM((T, D), BF16),
            pltpu.VMEM((MAX_POOL_WINDOW + T, POOL_WIDTH), F32),
            pltpu.VMEM((N_KV_HEADS, BLOCK + T, HEAD_DIM), BF16),
            pltpu.VMEM((KV_WIDTH, BLOCK + T), BF16),
            pltpu.VMEM((N_Q_HEADS, T, HEAD_DIM), BF16),
            pltpu.VMEM((ATTN_WIDTH, T), F32),
            pltpu.VMEM((T, ATTN_WIDTH), F32),
            pltpu.VMEM((T, D), BF16),
        ],
        compiler_params=pltpu.CompilerParams(
            dimension_semantics=("arbitrary", "arbitrary"),
            vmem_limit_bytes=VMEM_LIMIT_BYTES),
        name="hybrid_layer",
    )(params[0], x, *params[1:5], tab, *params[5:])


@jax.jit
def kernel(x, ln_g, ln_b, w_in, pool_w, pool_scale, sinks, w_out, rel_bias):
    params = (sinks.astype(F32), w_in, pool_w, pool_scale, w_out, ln_g, ln_b)
    tab = _bias_tables(rel_bias)
    for layer in range(DEPTH):
        x = _layer(layer, x, params, tab)
    return x
```

```python
import functools
import math

import jax
import jax.numpy as jnp
from jax import lax
from jax.experimental import pallas as pl
from jax.experimental.pallas import tpu as pltpu
import numpy as np

D_MODEL = 1024
DEPTH = 2
POOL_WIDTH = 512
POOL_WINDOWS = (2, 4, 8, 16)
POOL_GROUP = 128
MAX_POOL_WINDOW = 16
HEAD_DIM = 64
N_Q_HEADS = 8
N_KV_HEADS = 2
GQA_GROUP = 4
ATTN_WIDTH = 512
KV_WIDTH = 128
WINDOW = 128
BLOCK = 128
N_BUCKETS = 32
MAX_DISTANCE = 128
IN_WIDTH = 2304
COL_U, COL_GP, COL_Q, COL_K, COL_V, COL_GA = 0, 512, 1024, 1536, 1664, 1792
DEEPNORM_ALPHA = (2.0 * DEPTH) ** 0.25
LN_EPS = 1e-5
NEG_INF = -1e30
LOG2E = math.log2(math.e)

TILE = 1024
N_CHUNK = 256
LN_ROWS = 64
LOOKAHEAD = 5
HALF = BLOCK // 2
KEYS = BLOCK + HALF
OUT_SPLIT = 4
VMEM_LIMIT_BYTES = 56 * 1024 * 1024

F32 = jnp.float32
BF16 = jnp.bfloat16


def _silu(v):
    return v / (1.0 + jnp.exp2(v * (-LOG2E)))


def _layer_kernel(layer, sinks_ref, x_ref, w_in32_ref, pw_ref, pscale_ref, w_out32_ref, tab_ref,
                  ln_g_ref, ln_b_ref, o_ref,
                  w_in_ref, w_out_ref, xb_scr, uext_scr, kext_scr, vext_scr, q_scr, yattn_scr, gate_scr, y_scr):
    s_idx = pl.program_id(1)
    T = TILE
    H = MAX_POOL_WINDOW

    @pl.when((pl.program_id(0) == 0) & (s_idx == 0))
    def _():
        for c0 in range(0, IN_WIDTH, N_CHUNK):
            w = w_in32_ref[:, c0:c0 + N_CHUNK]
            if COL_Q <= c0 < COL_K:
                w = w * (HEAD_DIM ** -0.5)
            w_in_ref[:, c0:c0 + N_CHUNK] = w.astype(BF16)
        for c0 in range(0, D_MODEL, N_CHUNK):
            w_out_ref[:, c0:c0 + N_CHUNK] = w_out32_ref[:, c0:c0 + N_CHUNK].astype(BF16)

    uext_scr[0:H, :] = uext_scr[T:T + H, :]
    kext_scr[:, 0:BLOCK, :] = kext_scr[:, T:T + BLOCK, :]
    vext_scr[:, 0:BLOCK] = vext_scr[:, T:T + BLOCK]

    @pl.when(s_idx == 0)
    def _():
        uext_scr[0:H, :] = jnp.zeros((H, POOL_WIDTH), F32)
        kext_scr[:, 0:BLOCK, :] = jnp.zeros((N_KV_HEADS, BLOCK, HEAD_DIM), BF16)
        vext_scr[:, 0:BLOCK] = jnp.zeros((KV_WIDTH, BLOCK), BF16)

    xb_scr[...] = x_ref[...].astype(BF16)
    o_ref[...] = DEEPNORM_ALPHA * x_ref[...]

    def proj(col, width):
        return jnp.dot(xb_scr[...], w_in_ref[:, col:col + width], preferred_element_type=F32)

    def pool_u(p):
        c0 = p * N_CHUNK
        uext_scr[H:H + T, c0:c0 + N_CHUNK] = proj(COL_U + c0, N_CHUNK)

    def pool_mix(p):
        t_top = s_idx * T + lax.broadcasted_iota(jnp.int32, (H, POOL_GROUP), 0)
        c0 = p * N_CHUNK
        pooled = []
        for gi in range(2 * p, 2 * p + 2):
            w = POOL_WINDOWS[gi]
            g0 = gi * POOL_GROUP
            ext = uext_scr[:, g0:g0 + POOL_GROUP]
            acc, span = ext, 1
            while span < w:
                acc = acc + pltpu.roll(acc, span, axis=0)
                span *= 2
            top = acc[H:2 * H] / jnp.minimum(t_top + 1, w).astype(F32)
            mean = jnp.concatenate([top, acc[2 * H:H + T] * (1.0 / w)], axis=0)
            pooled.append((mean - ext[H:H + T]).astype(BF16))
        pooled = jnp.concatenate(pooled, axis=1)
        pw_a, pw_b = pw_ref[2 * p].astype(BF16), pw_ref[2 * p + 1].astype(BF16)
        zeros = jnp.zeros_like(pw_a)
        pw = jnp.concatenate([jnp.concatenate([pw_a, zeros], axis=1),
                              jnp.concatenate([zeros, pw_b], axis=1)], axis=0)
        mixed = jnp.dot(pooled, pw, preferred_element_type=F32)
        gate = _silu(proj(COL_GP + c0, N_CHUNK))
        y_scr[:, c0:c0 + N_CHUNK] = (mixed * pscale_ref[layer:layer + 1, c0:c0 + N_CHUNK] * gate).astype(BF16)

    def q_proj(c):
        q = proj(COL_Q + c * N_CHUNK, N_CHUNK).astype(BF16)
        for g in range(GQA_GROUP):
            q_scr[c * GQA_GROUP + g] = q[:, g * HEAD_DIM:(g + 1) * HEAD_DIM]

    def kv_proj():
        kv = proj(COL_K, 2 * KV_WIDTH)
        for h in range(N_KV_HEADS):
            kext_scr[h, BLOCK:BLOCK + T, :] = kv[:, h * HEAD_DIM:(h + 1) * HEAD_DIM].astype(BF16)
        vext_scr[:, BLOCK:BLOCK + T] = kv[:, KV_WIDTH:2 * KV_WIDTH].T.astype(BF16)

    def scores(h, i, a):
        r0 = i * BLOCK + a * HALF
        qs = q_scr[h * GQA_GROUP:(h + 1) * GQA_GROUP, r0:r0 + HALF, :]
        qs = qs.reshape(GQA_GROUP * HALF, HEAD_DIM)
        kwin = kext_scr[h, r0:r0 + KEYS, :]
        sc = lax.dot_general(kwin, qs, (((1,), (1,)), ((), ())),
                             preferred_element_type=F32)
        if i == 0:
            first = (s_idx == 0).astype(jnp.int32)
            return sc + tab_ref[first * (1 + a), h]
        return sc + tab_ref[0, h]

    def softmax_values(h, i, a, sc):
        sink = jnp.concatenate(
            [jnp.full((1, HALF), sinks_ref[layer, h * GQA_GROUP + g], F32) for g in range(GQA_GROUP)],
            axis=1)
        r0 = i * BLOCK
        vwin = vext_scr[h * HEAD_DIM:(h + 1) * HEAD_DIM, r0:r0 + 2 * BLOCK]
        m = jnp.maximum(jnp.max(sc, axis=0, keepdims=True), sink)
        e = jnp.exp(sc - m)
        l = jnp.sum(e, axis=0, keepdims=True) + jnp.exp(sink - m)
        e = e.astype(BF16)
        pad = jnp.zeros((HALF, GQA_GROUP * HALF), BF16)
        e = jnp.concatenate([e, pad] if a == 0 else [pad, e], axis=0)
        o = jnp.dot(vwin, e, preferred_element_type=F32)
        o = o * (1.0 / l)
        for g in range(GQA_GROUP):
            row = (h * GQA_GROUP + g) * HEAD_DIM
            yattn_scr[row:row + HEAD_DIM, r0 + a * HALF:r0 + (a + 1) * HALF] = o[:, g * HALF:(g + 1) * HALF]

    def gate_proj(c):
        c0 = c * N_CHUNK
        gate_scr[:, c0:c0 + N_CHUNK] = _silu(proj(COL_GA + c0, N_CHUNK))

    def gate_attention(c):
        c0 = c * N_CHUNK
        y_scr[:, POOL_WIDTH + c0:POOL_WIDTH + c0 + N_CHUNK] = (
            yattn_scr[c0:c0 + N_CHUNK, :].T * gate_scr[:, c0:c0 + N_CHUNK]).astype(BF16)

    def zero_of(v):
        bits = lax.bitcast_convert_type(v, jnp.uint32)
        bits = lax.shift_right_logical(lax.shift_right_logical(bits, jnp.uint32(16)), jnp.uint32(16))
        return lax.bitcast_convert_type(bits, F32)

    def finish():
        gamma = ln_g_ref[layer:layer + 1, :]
        beta = ln_b_ref[layer:layer + 1, :]
        group = T // OUT_SPLIT

        def project(hf):
            hrows = slice(hf * group, (hf + 1) * group)
            acc, marks = None, []
            for c0 in range(0, D_MODEL, N_CHUNK):
                blk = o_ref[hrows, c0:c0 + N_CHUNK] + jnp.dot(
                    y_scr[hrows, :], w_out_ref[:, c0:c0 + N_CHUNK], preferred_element_type=F32)
                o_ref[hrows, c0:c0 + N_CHUNK] = blk
                part = blk[:, :N_CHUNK // 2] + blk[:, N_CHUNK // 2:]
                acc = part if acc is None else acc + part
                marks.append(zero_of(blk[0:8, 0:1]))
            return jnp.sum(acc, axis=-1, keepdims=True) * (1.0 / D_MODEL), marks

        def normalise(hf, mu_g, marks):
            for r in range(group // LN_ROWS):
                rows = slice(hf * group + r * LN_ROWS, hf * group + (r + 1) * LN_ROWS)
                mu = mu_g[r * LN_ROWS:(r + 1) * LN_ROWS]
                if marks is not None:
                    mu = (mu.reshape(LN_ROWS // 8, 8, 1) + marks[r][None]).reshape(LN_ROWS, 1)
                zc = o_ref[rows, :] - mu
                var = jnp.mean(zc * zc, axis=-1, keepdims=True)
                o_ref[rows, :] = zc * lax.rsqrt(var + LN_EPS) * gamma + beta

        mu_prev, _ = project(0)
        for hf in range(1, OUT_SPLIT):
            mu_g, marks = project(hf)
            normalise(hf - 1, mu_prev, marks)
            mu_prev = mu_g
        normalise(OUT_SPLIT - 1, mu_prev, None)

    fillers = [lambda: pool_u(0), lambda: pool_u(1), lambda: pool_mix(0), lambda: gate_proj(0),
               lambda: pool_mix(1), lambda: gate_proj(1), lambda: gate_attention(0), lambda: None]
    blocks = [(h, i, a) for h in range(N_KV_HEADS) for i in range(T // BLOCK) for a in range(2)]
    kv_proj()
    q_proj(0)
    q_proj(1)
    pending = [scores(*blocks[k]) for k in range(LOOKAHEAD)]
    for k, (h, i, a) in enumerate(blocks):
        softmax_values(h, i, a, pending.pop(0))
        if k + LOOKAHEAD < len(blocks):
            pending.append(scores(*blocks[k + LOOKAHEAD]))
        if k % 4 == 3:
            fillers[k // 4]()
    gate_attention(1)
    finish()


def _t5_bucket(dist):
    max_exact = N_BUCKETS // 2
    ratio = (np.log(np.maximum(dist, 1).astype(np.float32) / np.float32(max_exact))
             / np.float32(math.log(MAX_DISTANCE / max_exact)))
    large = max_exact + (ratio * np.float32(N_BUCKETS - max_exact)).astype(np.int32)
    large = np.minimum(large, N_BUCKETS - 1)
    return np.where(dist < max_exact, dist, large)


def _table_kernel(by_dist_ref, tab_ref):
    lanes = 4 * BLOCK
    key = lax.broadcasted_iota(jnp.int32, (KEYS, HALF), 0)
    for hg in range(N_Q_HEADS):
        h, g = divmod(hg, GQA_GROUP)
        vec = jnp.concatenate([by_dist_ref[hg:hg + 1, :],
                               jnp.full((1, lanes - WINDOW), NEG_INF, F32)], axis=1)
        rolled = pltpu.roll(jnp.broadcast_to(vec, (KEYS, lanes)), 0, 1, stride=1, stride_axis=0)
        visible = rolled[:, BLOCK:BLOCK + HALF]
        cols = slice(g * HALF, (g + 1) * HALF)
        tab_ref[0, h, :, cols] = visible
        tab_ref[1, h, :, cols] = jnp.where(key >= BLOCK, visible, NEG_INF)
        tab_ref[2, h, :, cols] = jnp.where(key >= BLOCK - HALF, visible, NEG_INF)


def _bias_tables(rel_bias):
    onehot = _t5_bucket(np.arange(WINDOW))[:, None] == np.arange(N_BUCKETS)
    by_dist = jnp.sum(jnp.where(onehot[None], rel_bias.astype(F32).T[:, None, :], 0.0),
                      axis=-1)
    return pl.pallas_call(
        _table_kernel,
        out_shape=jax.ShapeDtypeStruct((3, N_KV_HEADS, KEYS, GQA_GROUP * HALF), F32),
        name="bias_tables",
    )(by_dist)


def _layer(layer, x, params, tab):
    B, S, D = x.shape
    T = TILE
    const = lambda *shape: pl.BlockSpec(shape, lambda b, s: (0,) * len(shape))
    of_layer = lambda *shape, **kw: pl.BlockSpec(
        (None,) + shape, lambda b, s: (layer,) + (0,) * len(shape), **kw)
    resident = dict(pipeline_mode=pl.Buffered(1))
    return pl.pallas_call(
        functools.partial(_layer_kernel, layer),
        out_shape=jax.ShapeDtypeStruct((B, S, D), F32),
        grid=(B, S // T),
        in_specs=[
            pl.BlockSpec(memory_space=pltpu.SMEM),
            pl.BlockSpec((None, T, D), lambda b, s: (b, s, 0)),
            of_layer(D, IN_WIDTH, **resident),
            of_layer(len(POOL_WINDOWS), POOL_GROUP, POOL_GROUP),
            const(DEPTH, POOL_WIDTH),
            of_layer(D, D, **resident),
            const(3, N_KV_HEADS, KEYS, GQA_GROUP * HALF),
            const(DEPTH, D),
            const(DEPTH, D),
        ],
        out_specs=pl.BlockSpec((None, T, D), lambda b, s: (b, s, 0)),
        scratch_shapes=[
            pltpu.VMEM((D, IN_WIDTH), BF16),
            pltpu.VMEM((D, D), BF16),
            pltpu.VMEM((T, D), BF16),
            pltpu.VMEM((MAX_POOL_WINDOW + T, POOL_WIDTH), F32),
            pltpu.VMEM((N_KV_HEADS, BLOCK + T, HEAD_DIM), BF16),
            pltpu.VMEM((KV_WIDTH, BLOCK + T), BF16),
            pltpu.VMEM((N_Q_HEADS, T, HEAD_DIM), BF16),
            pltpu.VMEM((ATTN_WIDTH, T), F32),
            pltpu.VMEM((T, ATTN_WIDTH), F32),
            pltpu.VMEM((T, D), BF16),
        ],
        compiler_params=pltpu.CompilerParams(
            dimension_semantics=("arbitrary", "arbitrary"),
            vmem_limit_bytes=VMEM_LIMIT_BYTES),
        name="hybrid_layer",
    )(params[0], x, *params[1:5], tab, *params[5:])


@jax.jit
def kernel(x, ln_g, ln_b, w_in, pool_w, pool_scale, sinks, w_out, rel_bias):
    params = (sinks.astype(F32), w_in, pool_w, pool_scale, w_out, ln_g, ln_b)
    tab = _bias_tables(rel_bias)
    for layer in range(DEPTH):
        x = _layer(layer, x, params, tab)
    return x
```
